```python
import jax, jax.numpy as jnp
from jax import lax
import numpy as np

D_MODEL = 4096
BATCH = 2
SEQ = 8192
DEPTH = 2

CONV_WIDTH = D_MODEL // 4
ATTN_WIDTH = D_MODEL // 2
POOL_WIDTH = D_MODEL // 4
HEAD_DIM = 64
N_Q_HEADS = ATTN_WIDTH // HEAD_DIM
N_KV_HEADS = N_Q_HEADS // 8
KV_WIDTH = N_KV_HEADS * HEAD_DIM
WINDOW = 128
BLOCK = 128
CONV_KERNEL = 31
POOL_WINDOWS = (2, 4, 8, 16)
N_POOL_GROUPS = 4
POOL_GROUP = POOL_WIDTH // N_POOL_GROUPS
NORM_EPS = 1e-5
LN_EPS = 1e-5

IN_SIZES = (CONV_WIDTH, CONV_WIDTH, CONV_WIDTH,
            ATTN_WIDTH, KV_WIDTH, KV_WIDTH, ATTN_WIDTH,
            POOL_WIDTH, POOL_WIDTH)
IN_WIDTH = sum(IN_SIZES)

kernel_name = "hybrid_conv_swa_pool_parallel_heads"


def _split_points():
    pts, acc = [], 0
    for s in IN_SIZES[:-1]:
        acc += s
        pts.append(acc)
    return pts


def rmsnorm(x, g):
    xf = x.astype(jnp.float32)
    y = xf * lax.rsqrt(jnp.mean(xf * xf, axis=-1, keepdims=True) + NORM_EPS)
    return (y * g.astype(jnp.float32)).astype(x.dtype)


def conformer_conv(a, b, dw, ln_g, ln_b, pw):
    u = a * jax.nn.sigmoid(b)
    y = lax.conv_general_dilated(
        u, dw[:, None, :].astype(u.dtype), window_strides=(1,),
        padding=[(CONV_KERNEL - 1, 0)],
        dimension_numbers=("NWC", "WIO", "NWC"),
        feature_group_count=CONV_WIDTH)
    yf = y.astype(jnp.float32)
    mu = jnp.mean(yf, axis=-1, keepdims=True)
    var = jnp.mean(jnp.square(yf - mu), axis=-1, keepdims=True)
    yn = (yf - mu) * lax.rsqrt(var + LN_EPS) * ln_g.astype(jnp.float32) + ln_b.astype(jnp.float32)
    return jax.nn.silu(yn).astype(a.dtype) @ pw


def window_attention(q, k, v, sinks):
    B, S, _ = q.shape
    nb = S // BLOCK
    G = N_Q_HEADS // N_KV_HEADS
    q = q.reshape(B, nb, BLOCK, N_KV_HEADS, G, HEAD_DIM)
    k = k.reshape(B, nb, BLOCK, N_KV_HEADS, HEAD_DIM)
    v = v.reshape(B, nb, BLOCK, N_KV_HEADS, HEAD_DIM)

    def with_prev(t):
        prev = jnp.concatenate([jnp.zeros_like(t[:, :1]), t[:, :-1]], axis=1)
        return jnp.concatenate([prev, t], axis=2)

    kk, vv = with_prev(k), with_prev(v)
    s = jnp.einsum("bnqkgd,bnskd->bnkgqs", q, kk,
                   preferred_element_type=jnp.float32) * (HEAD_DIM ** -0.5)
    qi = jnp.arange(BLOCK)[:, None]
    kj = jnp.arange(2 * BLOCK)[None, :]
    rel = qi + BLOCK - kj
    band = (rel >= 0) & (rel <= WINDOW)
    not_pad = (jnp.arange(nb)[:, None, None] > 0) | (kj[None] >= BLOCK)
    mask = band[None] & not_pad
    s = jnp.where(mask[None, :, None, None], s, -jnp.inf)
    sink = sinks.astype(jnp.float32).reshape(N_KV_HEADS, G)[None, None, :, :, None, None]
    m = jnp.maximum(jnp.max(s, axis=-1, keepdims=True), sink)
    p = jnp.exp(s - m)
    denom = jnp.sum(p, axis=-1, keepdims=True) + jnp.exp(sink - m)
    p = (p / denom).astype(v.dtype)
    o = jnp.einsum("bnkgqs,bnskd->bnqkgd", p, vv)
    return o.reshape(B, S, ATTN_WIDTH)


def multiscale_pool(u, w, scale):
    B, S, _ = u.shape
    uf = u.astype(jnp.float32).reshape(B, S, N_POOL_GROUPS, POOL_GROUP)
    cs = lax.cumsum(uf, axis=1)
    pos = jnp.arange(1, S + 1, dtype=jnp.float32)
    outs = []
    for g, win in enumerate(POOL_WINDOWS):
        c = cs[:, :, g]
        lagged = jnp.pad(c, ((0, 0), (win, 0), (0, 0)))[:, :S]
        mean = (c - lagged) / jnp.minimum(pos, float(win))[None, :, None]
        outs.append(mean - uf[:, :, g])
    mixed = jnp.stack(outs, axis=2).astype(u.dtype)
    y = jnp.einsum("bsgc,gcd->bsgd", mixed, w)
    return y.reshape(B, S, POOL_WIDTH) * scale


def hybrid_layer(x, norm_g, w_in, conv_dw, conv_ln_g, conv_ln_b, conv_pw,
                 attn_sinks, pool_w, pool_scale, w_out):
    h = rmsnorm(x, norm_g)
    proj = h @ w_in
    a, b, z_a, q, k, v, z_b, u_c, z_c = jnp.split(proj, _split_points(), axis=-1)
    y_a = conformer_conv(a, b, conv_dw, conv_ln_g, conv_ln_b, conv_pw) * jax.nn.silu(z_a)
    y_b = window_attention(q, k, v, attn_sinks) * jax.nn.silu(z_b)
    y_c = multiscale_pool(u_c, pool_w, pool_scale) * jax.nn.silu(z_c)
    y = jnp.concatenate([y_a, y_b, y_c], axis=-1)
    return x + y @ w_out


def setup_inputs(seed: int = 0) -> dict:
    key = jax.random.key(seed)
    ks = jax.random.split(key, 13)
    f32 = jnp.float32
    nrm = lambda k, shape: jax.random.normal(k, shape, dtype=f32)
    return {
        "x": nrm(ks[0], (BATCH, SEQ, D_MODEL)),
        "norm_g": 1.0 + 0.02 * nrm(ks[1], (DEPTH, D_MODEL)),
        "w_in": nrm(ks[2], (DEPTH, D_MODEL, IN_WIDTH)) * D_MODEL ** -0.5,
        "conv_dw": nrm(ks[3], (DEPTH, CONV_KERNEL, CONV_WIDTH)) * CONV_KERNEL ** -0.5,
        "conv_ln_g": 1.0 + 0.02 * nrm(ks[4], (DEPTH, CONV_WIDTH)),
        "conv_ln_b": 0.02 * nrm(ks[5], (DEPTH, CONV_WIDTH)),
        "conv_pw": nrm(ks[6], (DEPTH, CONV_WIDTH, CONV_WIDTH)) * CONV_WIDTH ** -0.5,
        "attn_sinks": nrm(ks[7], (DEPTH, N_Q_HEADS)),
        "pool_w": nrm(ks[8], (DEPTH, N_POOL_GROUPS, POOL_GROUP, POOL_GROUP)) * POOL_GROUP ** -0.5,
        "pool_scale": 1.0 + 0.02 * nrm(ks[9], (DEPTH, POOL_WIDTH)),
        "w_out": nrm(ks[10], (DEPTH, D_MODEL, D_MODEL)) * D_MODEL ** -0.5,
        "final_norm_g": 1.0 + 0.02 * nrm(ks[11], (D_MODEL,)),
    }


def reference(x, norm_g, w_in, conv_dw, conv_ln_g, conv_ln_b, conv_pw,
              attn_sinks, pool_w, pool_scale, w_out, final_norm_g):
    for l in range(DEPTH):
        x = hybrid_layer(x, norm_g[l], w_in[l], conv_dw[l], conv_ln_g[l], conv_ln_b[l],
                         conv_pw[l], attn_sinks[l], pool_w[l], pool_scale[l], w_out[l])
    return rmsnorm(x, final_norm_g)
```

```python
import functools

import jax
import jax.numpy as jnp
from jax import lax
from jax.experimental import pallas as pl
from jax.experimental.pallas import tpu as pltpu

F32 = jnp.float32
BF16 = jnp.bfloat16

D_MODEL = 4096
CONV_WIDTH = 1024
ATTN_WIDTH = 2048
POOL_WIDTH = 1024
HEAD_DIM = 64
N_Q_HEADS = 32
N_KV_HEADS = 4
Q_PER_KV = N_Q_HEADS // N_KV_HEADS
KV_WIDTH = N_KV_HEADS * HEAD_DIM
WINDOW = 128
BLOCK = 128
CONV_KERNEL = 31
POOL_WINDOWS = (2, 4, 8, 16)
POOL_GROUP = 256
NORM_EPS = 1e-5
LN_EPS = 1e-5
IN_WIDTH = 9728

LANES = 128
SUBLANES = 8
V7X_VMEM_BYTES = 64 * 1024 * 1024
VMEM_HEADROOM_BYTES = 6 * 1024 * 1024

PROJ_TILE = 512
PROJ_SRC_TILE = (6, 7, 8, 9, 11, 12, 13, 14, 0, 1, 2, 3, 4, 5, 15, 16, 17, 18, 10)
Q_BLK2048, ZB_BLK2048 = 0, 1
A_BLK1024, B_BLK1024, ZA_BLK1024, UC_BLK1024, ZC_BLK1024 = 4, 5, 6, 7, 8
KV_BLK512 = 18

CONV_HALO = 32
POOL_HALO = 16
CONV_STEPS = 16
NORM_ROWS = 32


def _vmem_limit(*block_bytes, scratch=0):
    need = 2 * sum(block_bytes) + scratch + VMEM_HEADROOM_BYTES
    return min(need, V7X_VMEM_BYTES - 2 * 1024 * 1024)


def _nbytes(shape, dtype):
    n = 1
    for s in shape:
        n *= s
    return n * jnp.dtype(dtype).itemsize


def _silu(z):
    return z * jax.nn.sigmoid(z)


def _prenorm_kernel(x_ref, g_ref, xg_ref, r_ref):
    x = x_ref[...]
    ssq = jnp.sum(x * x, axis=-1, keepdims=True)
    r = lax.rsqrt(ssq * (1.0 / D_MODEL) + NORM_EPS)
    xg_ref[...] = (x * g_ref[...]).astype(BF16)
    r_ref[...] = jnp.broadcast_to(r, r_ref.shape)


def _prenorm(x2, g, *, rows=256):
    t = x2.shape[0]
    return pl.pallas_call(
        _prenorm_kernel,
        grid=(t // rows,),
        in_specs=[pl.BlockSpec((rows, D_MODEL), lambda i: (i, 0)),
                  pl.BlockSpec((1, D_MODEL), lambda i: (0, 0))],
        out_specs=[pl.BlockSpec((rows, D_MODEL), lambda i: (i, 0)),
                   pl.BlockSpec((rows, LANES), lambda i: (i, 0))],
        out_shape=[jax.ShapeDtypeStruct((t, D_MODEL), BF16),
                   jax.ShapeDtypeStruct((t, LANES), F32)],
        compiler_params=pltpu.CompilerParams(
            dimension_semantics=("parallel",),
            vmem_limit_bytes=_vmem_limit(_nbytes((rows, D_MODEL), F32), _nbytes((rows, D_MODEL), BF16))),
        name="prenorm",
    )(x2, g.reshape(1, D_MODEL))


def _inproj_kernel(src_ref, xg_ref, w_ref, r_ref, o_ref):
    del src_ref
    acc = jnp.dot(xg_ref[...], w_ref[...], preferred_element_type=F32)
    r = jnp.tile(r_ref[...], (1, acc.shape[1] // LANES))
    o_ref[...] = (acc * r).astype(o_ref.dtype)


def _inproj(xg, w_bf, r, *, tm=2048):
    t = xg.shape[0]
    n_tiles = IN_WIDTH // PROJ_TILE
    src = jnp.asarray(PROJ_SRC_TILE, dtype=jnp.int32)
    grid_spec = pltpu.PrefetchScalarGridSpec(
        num_scalar_prefetch=1,
        grid=(t // tm, n_tiles),
        in_specs=[pl.BlockSpec((tm, D_MODEL), lambda i, j, s: (i, 0)),
                  pl.BlockSpec((D_MODEL, PROJ_TILE), lambda i, j, s: (0, s[j])),
                  pl.BlockSpec((tm, LANES), lambda i, j, s: (i, 0))],
        out_specs=pl.BlockSpec((tm, PROJ_TILE), lambda i, j, s: (i, j)),
    )
    return pl.pallas_call(
        _inproj_kernel,
        grid_spec=grid_spec,
        out_shape=jax.ShapeDtypeStruct((t, IN_WIDTH), BF16),
        compiler_params=pltpu.CompilerParams(
            dimension_semantics=("parallel", "arbitrary"),
            vmem_limit_bytes=_vmem_limit(_nbytes((tm, D_MODEL), BF16), _nbytes((D_MODEL, PROJ_TILE), BF16),
                                         _nbytes((tm, LANES), F32), _nbytes((tm, PROJ_TILE), BF16))),
        name="inproj",
    )(src, xg, w_bf, r)


def _slab_pitch(rows):
    p = -(-rows // SUBLANES)
    if p % 2 == 0:
        p += 1
    return p * SUBLANES


def _conv_kernel(a_ref, b_ref, za_ref, ah_ref, bh_ref, dw_ref, lng_ref, lnb_ref, pw_ref, o_ref,
                 u_scr, y_scr, s_scr, *, u_pitch, y_pitch):
    i = pl.program_id(1)
    ts = a_ref.shape[1]
    n_col = CONV_WIDTH // LANES
    for c in range(n_col):
        lanes = slice(c * LANES, (c + 1) * LANES)
        uh = ah_ref[0, :, lanes].astype(F32) * jax.nn.sigmoid(bh_ref[0, :, lanes].astype(F32))
        u_scr[c * u_pitch:c * u_pitch + CONV_HALO, :] = jnp.where(i > 0, uh, 0.0)
        u_scr[c * u_pitch + CONV_HALO:c * u_pitch + CONV_HALO + ts, :] = (
            a_ref[0, :, lanes].astype(F32) * jax.nn.sigmoid(b_ref[0, :, lanes].astype(F32)))

    first_tap = CONV_HALO - (CONV_KERNEL - 1)

    def conv_chunk(ci, carry):
        t0 = ci * CONV_STEPS
        taps_in = [u_scr[pl.ds(t0 + first_tap + k, n_col, stride=u_pitch), :]
                   for k in range(CONV_STEPS + CONV_KERNEL - 1)]
        for r in range(CONV_STEPS):
            acc = dw_ref[0] * taps_in[r]
            for j in range(1, CONV_KERNEL):
                acc = acc + dw_ref[j] * taps_in[r + j]
            y_scr[pl.ds(t0 + r, n_col, stride=y_pitch), :] = acc
        return carry

    lax.fori_loop(0, ts // CONV_STEPS, conv_chunk, 0)

    lng = lng_ref[...]
    lnb = lnb_ref[...]

    def norm_chunk(ci, carry):
        r0 = pl.multiple_of(ci * NORM_ROWS, NORM_ROWS)
        y = jnp.concatenate([y_scr[pl.ds(c * y_pitch + r0, NORM_ROWS), :] for c in range(n_col)], axis=1)
        mu = jnp.mean(y, axis=-1, keepdims=True)
        d = y - mu
        var = jnp.mean(d * d, axis=-1, keepdims=True)
        yn = d * lax.rsqrt(var + LN_EPS) * lng + lnb
        s_scr[pl.ds(r0, NORM_ROWS), :] = _silu(yn).astype(BF16)
        return carry

    lax.fori_loop(0, ts // NORM_ROWS, norm_chunk, 0)
    y = jnp.dot(s_scr[...], pw_ref[...], preferred_element_type=F32)
    o_ref[0] = (y * _silu(za_ref[0].astype(F32))).astype(o_ref.dtype)


def _conv_branch(proj3, dw, ln_g, ln_b, pw_bf, *, ts=512):
    b, s, _ = proj3.shape
    halo_per_blk = ts // CONV_HALO
    n_col = CONV_WIDTH // LANES
    u_pitch = _slab_pitch(CONV_HALO + ts)
    y_pitch = _slab_pitch(ts)
    cur = lambda blk: pl.BlockSpec((1, ts, CONV_WIDTH), lambda bi, i: (bi, i, blk))
    halo = lambda blk: pl.BlockSpec((1, CONV_HALO, CONV_WIDTH),
                                    lambda bi, i: (bi, jnp.maximum(i * halo_per_blk - 1, 0), blk))
    full = lambda shape: pl.BlockSpec(shape, lambda bi, i: (0,) * len(shape))
    scratch = [pltpu.VMEM((n_col * u_pitch, LANES), F32),
               pltpu.VMEM((n_col * y_pitch, LANES), F32),
               pltpu.VMEM((ts, CONV_WIDTH), BF16)]
    return pl.pallas_call(
        functools.partial(_conv_kernel, u_pitch=u_pitch, y_pitch=y_pitch),
        grid=(b, s // ts),
        in_specs=[cur(A_BLK1024), cur(B_BLK1024), cur(ZA_BLK1024), halo(A_BLK1024), halo(B_BLK1024),
                  full((CONV_KERNEL, n_col, LANES)), full((1, CONV_WIDTH)), full((1, CONV_WIDTH)),
                  full((CONV_WIDTH, CONV_WIDTH))],
        out_specs=pl.BlockSpec((1, ts, CONV_WIDTH), lambda bi, i: (bi, i, 0)),
        out_shape=jax.ShapeDtypeStruct((b, s, CONV_WIDTH), BF16),
        scratch_shapes=scratch,
        compiler_params=pltpu.CompilerParams(
            dimension_semantics=("parallel", "arbitrary"),
            vmem_limit_bytes=_vmem_limit(4 * _nbytes((ts, CONV_WIDTH), BF16),
                                         2 * _nbytes((CONV_HALO, CONV_WIDTH), BF16),
                                         _nbytes((CONV_WIDTH, CONV_WIDTH), BF16),
                                         _nbytes((CONV_KERNEL + 2, CONV_WIDTH), F32),
                                         scratch=sum(_nbytes(sc.shape, sc.dtype) for sc in scratch)
                                         + 2 * _nbytes((ts, CONV_WIDTH), F32))),
        name="conv_branch",
    )(proj3, proj3, proj3, proj3, proj3, dw.reshape(CONV_KERNEL, n_col, LANES),
      ln_g.reshape(1, -1), ln_b.reshape(1, -1), pw_bf)


def _attn_kernel(sink_ref, q_ref, zb_ref, kvc_ref, kvp_ref, o_ref):
    i = pl.program_id(1)
    tq = q_ref.shape[1]
    row = lax.broadcasted_iota(jnp.int32, (BLOCK, 2 * BLOCK), 0)
    col = lax.broadcasted_iota(jnp.int32, (BLOCK, 2 * BLOCK), 1)
    rel = row + BLOCK - col
    band = (rel >= 0) & (rel <= WINDOW)
    low = lax.broadcasted_iota(jnp.int32, (2 * BLOCK, LANES), 1) < HEAD_DIM
    low_q = lax.broadcasted_iota(jnp.int32, (BLOCK, LANES), 1) < HEAD_DIM
    ones_top = jnp.where(low, 1.0, 0.0).astype(BF16)
    ones_bot = jnp.where(low, 0.0, 1.0).astype(BF16)
    v_right = jnp.concatenate([ones_top, ones_bot], axis=0)

    for qb in range(tq // BLOCK):
        rows = slice(qb * BLOCK, (qb + 1) * BLOCK)
        if qb == 0:
            kv_prev = kvp_ref[0]
            valid = band & (col >= jnp.where(i == 0, BLOCK, 0))
        else:
            kv_prev = kvc_ref[0, (qb - 1) * BLOCK:qb * BLOCK, :]
            valid = band
        bias = jnp.where(valid, 0.0, -jnp.inf).astype(F32)
        bias2 = jnp.concatenate([bias, bias], axis=1)
        kv = jnp.concatenate([kv_prev, kvc_ref[0, rows, :]], axis=0).astype(F32)

        for h in range(N_KV_HEADS):
            def pair_layout(tile):
                swapped = pltpu.roll(tile, HEAD_DIM, axis=1)
                lo_src, hi_src = (tile, swapped) if h % 2 == 0 else (swapped, tile)
                return jnp.concatenate([jnp.where(low, lo_src, 0.0), jnp.where(low, 0.0, hi_src)], axis=0)

            k_tile = kv[:, (h // 2) * LANES:(h // 2 + 1) * LANES]
            v_tile = kv[:, KV_WIDTH + (h // 2) * LANES:KV_WIDTH + (h // 2 + 1) * LANES]
            k_bd = (pair_layout(k_tile) * (HEAD_DIM ** -0.5)).astype(BF16)
            v_bd = jnp.concatenate([pair_layout(v_tile).astype(BF16), v_right], axis=1)

            for p in range(Q_PER_KV // 2):
                head0 = h * Q_PER_KV + 2 * p
                lanes = slice(head0 * HEAD_DIM, (head0 + 2) * HEAD_DIM)
                s = lax.dot_general(q_ref[0, rows, lanes], k_bd, (((1,), (1,)), ((), ())),
                                    preferred_element_type=F32) + bias2
                sink0 = sink_ref[head0]
                sink1 = sink_ref[head0 + 1]
                m0 = jnp.maximum(jnp.max(s[:, :2 * BLOCK], axis=-1, keepdims=True), sink0)
                m1 = jnp.maximum(jnp.max(s[:, 2 * BLOCK:], axis=-1, keepdims=True), sink1)
                pexp = jnp.concatenate([jnp.exp(s[:, :2 * BLOCK] - m0), jnp.exp(s[:, 2 * BLOCK:] - m1)],
                                       axis=1).astype(BF16)
                o_ext = jnp.dot(pexp, v_bd, preferred_element_type=F32)
                sink_term = jnp.exp(jnp.where(low_q, sink0 - m0, sink1 - m1))
                attn = o_ext[:, :LANES] / (o_ext[:, LANES:] + sink_term)
                gate = _silu(zb_ref[0, rows, lanes].astype(F32))
                o_ref[0, rows, lanes] = (attn * gate).astype(o_ref.dtype)


def _attn_branch(proj3, sinks, *, tq=256):
    b, s, _ = proj3.shape
    blocks_per_step = tq // BLOCK
    return pl.pallas_call(
        _attn_kernel,
        grid=(b, s // tq),
        in_specs=[pl.BlockSpec(memory_space=pltpu.SMEM),
                  pl.BlockSpec((1, tq, ATTN_WIDTH), lambda bi, i: (bi, i, Q_BLK2048)),
                  pl.BlockSpec((1, tq, ATTN_WIDTH), lambda bi, i: (bi, i, ZB_BLK2048)),
                  pl.BlockSpec((1, tq, 2 * KV_WIDTH), lambda bi, i: (bi, i, KV_BLK512)),
                  pl.BlockSpec((1, BLOCK, 2 * KV_WIDTH),
                               lambda bi, i: (bi, jnp.maximum(i * blocks_per_step - 1, 0), KV_BLK512))],
        out_specs=pl.BlockSpec((1, tq, ATTN_WIDTH), lambda bi, i: (bi, i, 0)),
        out_shape=jax.ShapeDtypeStruct((b, s, ATTN_WIDTH), BF16),
        compiler_params=pltpu.CompilerParams(
            dimension_semantics=("parallel", "arbitrary"),
            vmem_limit_bytes=_vmem_limit(3 * _nbytes((tq, ATTN_WIDTH), BF16),
                                         2 * _nbytes((tq, 2 * KV_WIDTH), BF16),
                                         scratch=8 * 1024 * 1024)),
        name="attn_branch",
    )(sinks, proj3, proj3, proj3, proj3)


def _pool_kernel(u_ref, z_ref, uh_ref, w_ref, sc_ref, o_ref, u_scr):
    i = pl.program_id(1)
    ts = u_ref.shape[1]
    u_scr[0:POOL_HALO, :] = jnp.where(i > 0, uh_ref[0].astype(F32), 0.0)
    u_scr[POOL_HALO:POOL_HALO + ts, :] = u_ref[0].astype(F32)
    pos = (lax.broadcasted_iota(jnp.int32, (ts, POOL_GROUP), 0) + (i * ts + 1)).astype(F32)
    for g, win in enumerate(POOL_WINDOWS):
        cols = slice(g * POOL_GROUP, (g + 1) * POOL_GROUP)
        cur = u_scr[POOL_HALO:POOL_HALO + ts, cols]
        tot = cur
        for k in range(1, win):
            tot = tot + u_scr[POOL_HALO - k:POOL_HALO - k + ts, cols]
        mixed = (tot / jnp.minimum(pos, float(win)) - cur).astype(BF16)
        y = jnp.dot(mixed, w_ref[g], preferred_element_type=F32)
        gate = _silu(z_ref[0, :, cols].astype(F32))
        o_ref[0, :, cols] = (y * sc_ref[:, cols] * gate).astype(o_ref.dtype)


def _pool_branch(proj3, pool_w_bf, pool_scale, *, ts=512):
    b, s, _ = proj3.shape
    halo_per_blk = ts // POOL_HALO
    return pl.pallas_call(
        _pool_kernel,
        grid=(b, s // ts),
        in_specs=[pl.BlockSpec((1, ts, POOL_WIDTH), lambda bi, i: (bi, i, UC_BLK1024)),
                  pl.BlockSpec((1, ts, POOL_WIDTH), lambda bi, i: (bi, i, ZC_BLK1024)),
                  pl.BlockSpec((1, POOL_HALO, POOL_WIDTH),
                               lambda bi, i: (bi, jnp.maximum(i * halo_per_blk - 1, 0), UC_BLK1024)),
                  pl.BlockSpec((len(POOL_WINDOWS), POOL_GROUP, POOL_GROUP), lambda bi, i: (0, 0, 0)),
                  pl.BlockSpec((1, POOL_WIDTH), lambda bi, i: (0, 0))],
        out_specs=pl.BlockSpec((1, ts, POOL_WIDTH), lambda bi, i: (bi, i, 0)),
        out_shape=jax.ShapeDtypeStruct((b, s, POOL_WIDTH), BF16),
        scratch_shapes=[pltpu.VMEM((POOL_HALO + ts, POOL_WIDTH), F32)],
        compiler_params=pltpu.CompilerParams(
            dimension_semantics=("parallel", "arbitrary"),
            vmem_limit_bytes=_vmem_limit(3 * _nbytes((ts, POOL_WIDTH), BF16),
                                         scratch=_nbytes((POOL_HALO + ts, POOL_WIDTH), F32)
                                         + 4 * _nbytes((ts, POOL_WIDTH), F32))),
        name="pool_branch",
    )(proj3, proj3, proj3, pool_w_bf, pool_scale.reshape(1, -1))


def _outproj_kernel(ya_ref, yb_ref, yc_ref, w_ref, x_ref, g_ref, *rest, final):
    j = pl.program_id(1)
    nj = pl.num_programs(1)
    tn = w_ref.shape[1]
    acc = jnp.dot(ya_ref[...], w_ref[0:CONV_WIDTH, :], preferred_element_type=F32)
    acc = acc + jnp.dot(yb_ref[...], w_ref[CONV_WIDTH:CONV_WIDTH + ATTN_WIDTH, :], preferred_element_type=F32)
    acc = acc + jnp.dot(yc_ref[...], w_ref[CONV_WIDTH + ATTN_WIDTH:, :], preferred_element_type=F32)
    xn = x_ref[...] + acc
    sq = xn * xn
    part = sq[:, 0:LANES]
    for c in range(1, tn // LANES):
        part = part + sq[:, c * LANES:(c + 1) * LANES]

    if final:
        o_ref, ssq_scr = rest
    else:
        xn_ref, xg_ref, r_ref, ssq_scr = rest

    @pl.when(j == 0)
    def _():
        ssq_scr[...] = part

    @pl.when(j > 0)
    def _():
        ssq_scr[...] = ssq_scr[...] + part

    def row_scale():
        ssq = jnp.sum(ssq_scr[...], axis=-1, keepdims=True)
        return lax.rsqrt(ssq * (1.0 / D_MODEL) + NORM_EPS)

    if final:
        for jj in range(D_MODEL // tn):
            @pl.when(j == jj)
            def _(jj=jj):
                o_ref[:, jj * tn:(jj + 1) * tn] = xn

        @pl.when(j == nj - 1)
        def _():
            r = row_scale()
            for jj in range(D_MODEL // tn):
                cols = slice(jj * tn, (jj + 1) * tn)
                o_ref[:, cols] = o_ref[:, cols] * r * g_ref[:, cols]
    else:
        xn_ref[...] = xn
        xg_ref[...] = (xn * g_ref[...]).astype(BF16)

        @pl.when(j == nj - 1)
        def _():
            r_ref[...] = jnp.broadcast_to(row_scale(), r_ref.shape)


def _outproj(ya, yb, yc, w_bf, x2, g, *, final, tm, tn):
    t = x2.shape[0]
    grid = (t // tm, D_MODEL // tn)
    in_specs = [pl.BlockSpec((tm, CONV_WIDTH), lambda i, j: (i, 0)),
                pl.BlockSpec((tm, ATTN_WIDTH), lambda i, j: (i, 0)),
                pl.BlockSpec((tm, POOL_WIDTH), lambda i, j: (i, 0)),
                pl.BlockSpec((D_MODEL, tn), lambda i, j: (0, j)),
                pl.BlockSpec((tm, tn), lambda i, j: (i, j))]
    blocks = [_nbytes((tm, D_MODEL), BF16), _nbytes((D_MODEL, tn), BF16), _nbytes((tm, tn), F32)]
    if final:
        in_specs.append(pl.BlockSpec((1, D_MODEL), lambda i, j: (0, 0)))
        out_specs = pl.BlockSpec((tm, D_MODEL), lambda i, j: (i, 0))
        out_shape = jax.ShapeDtypeStruct((t, D_MODEL), F32)
        blocks.append(_nbytes((tm, D_MODEL), F32))
    else:
        in_specs.append(pl.BlockSpec((1, tn), lambda i, j: (0, j)))
        out_specs = [pl.BlockSpec((tm, tn), lambda i, j: (i, j)),
                     pl.BlockSpec((tm, tn), lambda i, j: (i, j)),
                     pl.BlockSpec((tm, LANES), lambda i, j: (i, 0))]
        out_shape = [jax.ShapeDtypeStruct((t, D_MODEL), F32),
                     jax.ShapeDtypeStruct((t, D_MODEL), BF16),
                     jax.ShapeDtypeStruct((t, LANES), F32)]
        blocks += [_nbytes((tm, tn), F32), _nbytes((tm, tn), BF16)]
    return pl.pallas_call(
        functools.partial(_outproj_kernel, final=final),
        grid=grid,
        in_specs=in_specs,
        out_specs=out_specs,
        out_shape=out_shape,
        scratch_shapes=[pltpu.VMEM((tm, LANES), F32)],
        compiler_params=pltpu.CompilerParams(
            dimension_semantics=("parallel", "arbitrary"),
            vmem_limit_bytes=_vmem_limit(*blocks, scratch=2 * _nbytes((tm, tn), F32))),
        name="outproj_final" if final else "outproj",
    )(ya, yb, yc, w_bf, x2, g.reshape(1, D_MODEL))


def kernel(x, norm_g, w_in, conv_dw, conv_ln_g, conv_ln_b, conv_pw, attn_sinks, pool_w, pool_scale,
           w_out, final_norm_g):
    b, s, d = x.shape
    assert d == D_MODEL and w_in.shape[-1] == IN_WIDTH and s % 512 == 0
    depth = w_in.shape[0]
    t = b * s
    w_in_bf = w_in.astype(BF16)
    w_out_bf = w_out.astype(BF16)
    conv_pw_bf = conv_pw.astype(BF16)
    pool_w_bf = pool_w.astype(BF16)

    x2 = x.reshape(t, d)
    xg, r = _prenorm(x2, norm_g[0])
    for l in range(depth):
        proj3 = _inproj(xg, w_in_bf[l], r).reshape(b, s, IN_WIDTH)
        ya = _conv_branch(proj3, conv_dw[l], conv_ln_g[l], conv_ln_b[l], conv_pw_bf[l]).reshape(t, CONV_WIDTH)
        yb = _attn_branch(proj3, attn_sinks[l]).reshape(t, ATTN_WIDTH)
        yc = _pool_branch(proj3, pool_w_bf[l], pool_scale[l]).reshape(t, POOL_WIDTH)
        if l + 1 < depth:
            x2, xg, r = _outproj(ya, yb, yc, w_out_bf[l], x2, norm_g[l + 1], final=False, tm=1024, tn=512)
        else:
            x2 = _outproj(ya, yb, yc, w_out_bf[l], x2, final_norm_g, final=True, tm=512, tn=512)
    return x2.reshape(b, s, d)
```

```python
import functools

import jax
import jax.numpy as jnp
from jax import lax
from jax.experimental import pallas as pl
from jax.experimental.pallas import tpu as pltpu

F32 = jnp.float32
BF16 = jnp.bfloat16

D_MODEL = 4096
CONV_WIDTH = 1024
ATTN_WIDTH = 2048
POOL_WIDTH = 1024
HEAD_DIM = 64
N_Q_HEADS = 32
N_KV_HEADS = 4
Q_PER_KV = N_Q_HEADS // N_KV_HEADS
KV_WIDTH = N_KV_HEADS * HEAD_DIM
WINDOW = 128
BLOCK = 128
CONV_KERNEL = 31
POOL_WINDOWS = (2, 4, 8, 16)
POOL_GROUP = 256
NORM_EPS = 1e-5
LN_EPS = 1e-5
IN_WIDTH = 9728
LOG2E = 1.4426950408889634

LANES = 128
SUBLANES = 8
V7X_VMEM_BYTES = 64 * 1024 * 1024
VMEM_HEADROOM_BYTES = 6 * 1024 * 1024

PROJ_TILE = 512
PROJ_SRC_TILE = (6, 7, 8, 9, 11, 12, 13, 14, 0, 1, 2, 3, 4, 5, 15, 16, 17, 18, 10)
Q_BLK2048, ZB_BLK2048 = 0, 1
A_BLK1024, B_BLK1024, ZA_BLK1024, UC_BLK1024, ZC_BLK1024 = 4, 5, 6, 7, 8
KV_BLK512 = 18

CONV_HALO = 32
POOL_HALO = 16
CONV_STEPS = 64
NORM_ROWS = 32
NORM_UNROLL = 4


def _vmem_limit(*block_bytes, scratch=0):
    need = 2 * sum(block_bytes) + scratch + VMEM_HEADROOM_BYTES
    return min(need, V7X_VMEM_BYTES - 2 * 1024 * 1024)


def _nbytes(shape, dtype):
    n = 1
    for s in shape:
        n *= s
    return n * jnp.dtype(dtype).itemsize


def _silu(z):
    return z * jax.nn.sigmoid(z)


def _prenorm_kernel(x_ref, g_ref, xg_ref, r_ref):
    x = x_ref[...]
    ssq = jnp.sum(x * x, axis=-1, keepdims=True)
    r = lax.rsqrt(ssq * (1.0 / D_MODEL) + NORM_EPS)
    xg_ref[...] = (x * g_ref[...]).astype(BF16)
    r_ref[...] = jnp.broadcast_to(r, r_ref.shape)


def _prenorm(x2, g, *, rows=256):
    t = x2.shape[0]
    return pl.pallas_call(
        _prenorm_kernel,
        grid=(t // rows,),
        in_specs=[pl.BlockSpec((rows, D_MODEL), lambda i: (i, 0)),
                  pl.BlockSpec((1, D_MODEL), lambda i: (0, 0))],
        out_specs=[pl.BlockSpec((rows, D_MODEL), lambda i: (i, 0)),
                   pl.BlockSpec((rows, LANES), lambda i: (i, 0))],
        out_shape=[jax.ShapeDtypeStruct((t, D_MODEL), BF16),
                   jax.ShapeDtypeStruct((t, LANES), F32)],
        compiler_params=pltpu.CompilerParams(
            dimension_semantics=("parallel",),
            vmem_limit_bytes=_vmem_limit(_nbytes((rows, D_MODEL), F32), _nbytes((rows, D_MODEL), BF16))),
        name="prenorm",
    )(x2, g.reshape(1, D_MODEL))


def _inproj_kernel(src_ref, xg_ref, w_ref, r_ref, o_ref):
    del src_ref
    acc = jnp.dot(xg_ref[...], w_ref[...], preferred_element_type=F32)
    r = jnp.tile(r_ref[...], (1, acc.shape[1] // LANES))
    o_ref[...] = (acc * r).astype(o_ref.dtype)


def _inproj(xg, w_bf, layer, r, *, tm=2048):
    t = xg.shape[0]
    n_tiles = IN_WIDTH // PROJ_TILE
    src = jnp.asarray(PROJ_SRC_TILE, dtype=jnp.int32)
    grid_spec = pltpu.PrefetchScalarGridSpec(
        num_scalar_prefetch=1,
        grid=(t // tm, n_tiles),
        in_specs=[pl.BlockSpec((tm, D_MODEL), lambda i, j, s: (i, 0)),
                  pl.BlockSpec((None, D_MODEL, PROJ_TILE), lambda i, j, s: (layer, 0, s[j])),
                  pl.BlockSpec((tm, LANES), lambda i, j, s: (i, 0))],
        out_specs=pl.BlockSpec((tm, PROJ_TILE), lambda i, j, s: (i, j)),
    )
    return pl.pallas_call(
        _inproj_kernel,
        grid_spec=grid_spec,
        out_shape=jax.ShapeDtypeStruct((t, IN_WIDTH), BF16),
        compiler_params=pltpu.CompilerParams(
            dimension_semantics=("parallel", "arbitrary"),
            vmem_limit_bytes=_vmem_limit(_nbytes((tm, D_MODEL), BF16), _nbytes((D_MODEL, PROJ_TILE), BF16),
                                         _nbytes((tm, LANES), F32), _nbytes((tm, PROJ_TILE), BF16))),
        name="inproj",
    )(src, xg, w_bf, r)


def _slab_pitch(rows):
    p = -(-rows // SUBLANES)
    if p % 2 == 0:
        p += 1
    return p * SUBLANES


def _conv_kernel(a_ref, b_ref, za_ref, ah_ref, bh_ref, dw_ref, lng_ref, lnb_ref, pw_ref, o_ref,
                 u_scr, y_scr, s_scr, *, u_pitch, y_pitch):
    i = pl.program_id(1)
    ts = a_ref.shape[1]
    n_col = CONV_WIDTH // LANES
    for c in range(n_col):
        lanes = slice(c * LANES, (c + 1) * LANES)
        uh = ah_ref[0, :, lanes].astype(F32) * jax.nn.sigmoid(bh_ref[0, :, lanes].astype(F32))
        u_scr[c * u_pitch:c * u_pitch + CONV_HALO, :] = jnp.where(i > 0, uh, 0.0)
        u_scr[c * u_pitch + CONV_HALO:c * u_pitch + CONV_HALO + ts, :] = (
            a_ref[0, :, lanes].astype(F32) * jax.nn.sigmoid(b_ref[0, :, lanes].astype(F32)))

    first_tap = CONV_HALO - (CONV_KERNEL - 1)

    def conv_chunk(ci, carry):
        t0 = ci * CONV_STEPS
        taps_in = [u_scr[pl.ds(t0 + first_tap + k, n_col, stride=u_pitch), :]
                   for k in range(CONV_STEPS + CONV_KERNEL - 1)]
        for r in range(CONV_STEPS):
            acc = dw_ref[0] * taps_in[r]
            for j in range(1, CONV_KERNEL):
                acc = acc + dw_ref[j] * taps_in[r + j]
            y_scr[pl.ds(t0 + r, n_col, stride=y_pitch), :] = acc
        return carry

    lax.fori_loop(0, ts // CONV_STEPS, conv_chunk, 0)

    lng = lng_ref[...]
    lnb = lnb_ref[...]

    def norm_chunk(ci, carry):
        r0 = pl.multiple_of(ci * NORM_ROWS, NORM_ROWS)
        y = jnp.concatenate([y_scr[pl.ds(c * y_pitch + r0, NORM_ROWS), :] for c in range(n_col)], axis=1)
        mu = jnp.mean(y, axis=-1, keepdims=True)
        d = y - mu
        var = jnp.mean(d * d, axis=-1, keepdims=True)
        yn = d * lax.rsqrt(var + LN_EPS) * lng + lnb
        s_scr[pl.ds(r0, NORM_ROWS), :] = _silu(yn).astype(BF16)
        return carry

    lax.fori_loop(0, ts // NORM_ROWS, norm_chunk, 0, unroll=NORM_UNROLL)
    y = jnp.dot(s_scr[...], pw_ref[...], preferred_element_type=F32)
    o_ref[0] = (y * _silu(za_ref[0].astype(F32))).astype(o_ref.dtype)


def _conv_branch(proj3, dw, ln_g, ln_b, pw_bf, *, ts=512):
    b, s, _ = proj3.shape
    halo_per_blk = ts // CONV_HALO
    n_col = CONV_WIDTH // LANES
    u_pitch = _slab_pitch(CONV_HALO + ts)
    y_pitch = _slab_pitch(ts)
    cur = lambda blk: pl.BlockSpec((1, ts, CONV_WIDTH), lambda bi, i: (bi, i, blk))
    halo = lambda blk: pl.BlockSpec((1, CONV_HALO, CONV_WIDTH),
                                    lambda bi, i: (bi, jnp.maximum(i * halo_per_blk - 1, 0), blk))
    full = lambda shape: pl.BlockSpec(shape, lambda bi, i: (0,) * len(shape))
    scratch = [pltpu.VMEM((n_col * u_pitch, LANES), F32),
               pltpu.VMEM((n_col * y_pitch, LANES), F32),
               pltpu.VMEM((ts, CONV_WIDTH), BF16)]
    return pl.pallas_call(
        functools.partial(_conv_kernel, u_pitch=u_pitch, y_pitch=y_pitch),
        grid=(b, s // ts),
        in_specs=[cur(A_BLK1024), cur(B_BLK1024), cur(ZA_BLK1024), halo(A_BLK1024), halo(B_BLK1024),
                  full((CONV_KERNEL, n_col, LANES)), full((1, CONV_WIDTH)), full((1, CONV_WIDTH)),
                  full((CONV_WIDTH, CONV_WIDTH))],
        out_specs=pl.BlockSpec((1, ts, CONV_WIDTH), lambda bi, i: (bi, i, 0)),
        out_shape=jax.ShapeDtypeStruct((b, s, CONV_WIDTH), BF16),
        scratch_shapes=scratch,
        compiler_params=pltpu.CompilerParams(
            dimension_semantics=("parallel", "arbitrary"),
            vmem_limit_bytes=_vmem_limit(4 * _nbytes((ts, CONV_WIDTH), BF16),
                                         2 * _nbytes((CONV_HALO, CONV_WIDTH), BF16),
                                         _nbytes((CONV_WIDTH, CONV_WIDTH), BF16),
                                         _nbytes((CONV_KERNEL + 2, CONV_WIDTH), F32),
                                         scratch=sum(_nbytes(sc.shape, sc.dtype) for sc in scratch)
                                         + 2 * _nbytes((ts, CONV_WIDTH), F32))),
        name="conv_branch",
    )(proj3, proj3, proj3, proj3, proj3, dw.reshape(CONV_KERNEL, n_col, LANES),
      ln_g.reshape(1, -1), ln_b.reshape(1, -1), pw_bf)


def _attn_kernel(sink_ref, q_ref, zb_ref, kvc_ref, kvp_ref, o_ref):
    i = pl.program_id(1)
    tq = q_ref.shape[1]
    row = lax.broadcasted_iota(jnp.int32, (BLOCK, 2 * BLOCK), 0)
    col = lax.broadcasted_iota(jnp.int32, (BLOCK, 2 * BLOCK), 1)
    rel = row + BLOCK - col
    band = (rel >= 0) & (rel <= WINDOW)
    low = lax.broadcasted_iota(jnp.int32, (2 * BLOCK, LANES), 1) < HEAD_DIM
    low_q = lax.broadcasted_iota(jnp.int32, (BLOCK, LANES), 1) < HEAD_DIM
    ones_top = jnp.where(low, 1.0, 0.0).astype(BF16)
    ones_bot = jnp.where(low, 0.0, 1.0).astype(BF16)
    v_right = jnp.concatenate([ones_top, ones_bot], axis=0)

    for qb in range(tq // BLOCK):
        rows = slice(qb * BLOCK, (qb + 1) * BLOCK)
        if qb == 0:
            kv_prev = kvp_ref[0]
            valid = band & (col >= jnp.where(i == 0, BLOCK, 0))
        else:
            kv_prev = kvc_ref[0, (qb - 1) * BLOCK:qb * BLOCK, :]
            valid = band
        bias = jnp.where(valid, 0.0, -jnp.inf).astype(F32)
        bias2 = jnp.concatenate([bias, bias], axis=1)
        kv =jnp.concatenate([kv_prev, kvc_ref[0, rows, :]], axis=0).astype(F32)

        for h in range(N_KV_HEADS):
            def pair_layout(tile):
                swapped = pltpu.roll(tile, HEAD_DIM, axis=1)
                lo_src, hi_src = (tile, swapped) if h % 2 == 0 else (swapped, tile)
                return jnp.concatenate([jnp.where(low, lo_src, 0.0), jnp.where(low, 0.0, hi_src)], axis=0)

            k_tile = kv[:, (h // 2) * LANES:(h // 2 + 1) * LANES]
            v_tile = kv[:, KV_WIDTH + (h // 2) * LANES:KV_WIDTH + (h // 2 + 1) * LANES]
            k_bd = (pair_layout(k_tile) * (HEAD_DIM ** -0.5 * LOG2E)).astype(BF16)
            v_bd = jnp.concatenate([pair_layout(v_tile).astype(BF16), v_right], axis=1)

            for p in range(Q_PER_KV // 2):
                head0 = h * Q_PER_KV + 2 * p
                lanes = slice(head0 * HEAD_DIM, (head0 + 2) * HEAD_DIM)
                s = lax.dot_general(q_ref[0, rows, lanes], k_bd, (((1,), (1,)), ((), ())),
                                    preferred_element_type=F32) + bias2
                m0 = jnp.max(s[:, :2 * BLOCK], axis=-1, keepdims=True)
                m1 = jnp.max(s[:, 2 * BLOCK:], axis=-1, keepdims=True)
                pexp = jnp.concatenate([jnp.exp2(s[:, :2 * BLOCK] - m0), jnp.exp2(s[:, 2 * BLOCK:] - m1)],
                                       axis=1).astype(BF16)
                o_ext = jnp.dot(pexp, v_bd, preferred_element_type=F32)
                sink2 = jnp.where(low_q[0:1], sink_ref[head0] * LOG2E, sink_ref[head0 + 1] * LOG2E)
                sink_term = jnp.exp2(sink2 - jnp.where(low_q, m0, m1))
                attn = o_ext[:, :LANES] / (o_ext[:, LANES:] + sink_term)
                gate = _silu(zb_ref[0, rows, lanes].astype(F32))
                o_ref[0, rows, lanes] = (attn * gate).astype(o_ref.dtype)


def _attn_branch(proj3, sinks, *, tq=256):
    b, s, _ = proj3.shape
    blocks_per_step = tq // BLOCK
    return pl.pallas_call(
        _attn_kernel,
        grid=(b, s // tq),
        in_specs=[pl.BlockSpec(memory_space=pltpu.SMEM),
                  pl.BlockSpec((1, tq, ATTN_WIDTH), lambda bi, i: (bi, i, Q_BLK2048)),
                  pl.BlockSpec((1, tq, ATTN_WIDTH), lambda bi, i: (bi, i, ZB_BLK2048)),
                  pl.BlockSpec((1, tq, 2 * KV_WIDTH), lambda bi, i: (bi, i, KV_BLK512)),
                  pl.BlockSpec((1, BLOCK, 2 * KV_WIDTH),
                               lambda bi, i: (bi, jnp.maximum(i * blocks_per_step - 1, 0), KV_BLK512))],
        out_specs=pl.BlockSpec((1, tq, ATTN_WIDTH), lambda bi, i: (bi, i, 0)),
        out_shape=jax.ShapeDtypeStruct((b, s, ATTN_WIDTH), BF16),
        compiler_params=pltpu.CompilerParams(
            dimension_semantics=("parallel", "arbitrary"),
            vmem_limit_bytes=_vmem_limit(3 * _nbytes((tq, ATTN_WIDTH), BF16),
                                         2 * _nbytes((tq, 2 * KV_WIDTH), BF16),
                                         scratch=8 * 1024 * 1024)),
        name="attn_branch",
    )(sinks, proj3, proj3, proj3, proj3)


def _pool_kernel(u_ref, z_ref, uh_ref, w_ref, sc_ref, o_ref, u_scr):
    i = pl.program_id(1)
    ts = u_ref.shape[1]
    u_scr[0:POOL_HALO, :] = jnp.where(i > 0, uh_ref[0], jnp.zeros_like(uh_ref[0]))
    u_scr[POOL_HALO:POOL_HALO + ts, :] = u_ref[0]
    k_rows = BLOCK + POOL_HALO
    rel = (lax.broadcasted_iota(jnp.int32, (BLOCK, k_rows), 0) + POOL_HALO
           - lax.broadcasted_iota(jnp.int32, (BLOCK, k_rows), 1))
    pos0 = lax.broadcasted_iota(jnp.int32, (BLOCK, POOL_GROUP), 0) + (i * ts + 1)
    for g, win in enumerate(POOL_WINDOWS):
        cols = slice(g * POOL_GROUP, (g + 1) * POOL_GROUP)
        band = jnp.where((rel >= 0) & (rel < win), 1.0, 0.0).astype(BF16)
        mixed = []
        for rb in range(ts // BLOCK):
            tot = jnp.dot(band, u_scr[rb * BLOCK:rb * BLOCK + k_rows, cols], preferred_element_type=F32)
            cur = u_ref[0, rb * BLOCK:(rb + 1) * BLOCK, cols].astype(F32)
            cnt = jnp.minimum((pos0 + rb * BLOCK).astype(F32), float(win))
            mixed.append((tot / cnt - cur).astype(BF16))
        y = jnp.dot(jnp.concatenate(mixed, axis=0), w_ref[g], preferred_element_type=F32)
        gate = _silu(z_ref[0, :, cols].astype(F32))
        o_ref[0, :, cols] = (y * sc_ref[:, cols] * gate).astype(o_ref.dtype)


def _pool_branch(proj3, pool_w_bf, pool_scale, *, ts=512):
    b, s, _ = proj3.shape
    halo_per_blk = ts // POOL_HALO
    return pl.pallas_call(
        _pool_kernel,
        grid=(b, s // ts),
        in_specs=[pl.BlockSpec((1, ts, POOL_WIDTH), lambda bi, i: (bi, i, UC_BLK1024)),
                  pl.BlockSpec((1, ts, POOL_WIDTH), lambda bi, i: (bi, i, ZC_BLK1024)),
                  pl.BlockSpec((1, POOL_HALO, POOL_WIDTH),
                               lambda bi, i: (bi, jnp.maximum(i * halo_per_blk - 1, 0), UC_BLK1024)),
                  pl.BlockSpec((len(POOL_WINDOWS), POOL_GROUP, POOL_GROUP), lambda bi, i: (0, 0, 0)),
                  pl.BlockSpec((1, POOL_WIDTH), lambda bi, i: (0, 0))],
        out_specs=pl.BlockSpec((1, ts, POOL_WIDTH), lambda bi, i: (bi, i, 0)),
        out_shape=jax.ShapeDtypeStruct((b, s, POOL_WIDTH), BF16),
        scratch_shapes=[pltpu.VMEM((POOL_HALO + ts, POOL_WIDTH), BF16)],
        compiler_params=pltpu.CompilerParams(
            dimension_semantics=("parallel", "arbitrary"),
            vmem_limit_bytes=_vmem_limit(3 * _nbytes((ts, POOL_WIDTH), BF16),
                                         scratch=_nbytes((POOL_HALO + ts, POOL_WIDTH), BF16)
                                         + 4 * _nbytes((ts, POOL_WIDTH), F32))),
        name="pool_branch",
    )(proj3, proj3, proj3, pool_w_bf, pool_scale.reshape(1, -1))


def _outproj_kernel(ya_ref, yb_ref, yc_ref, w_ref, x_ref, g_ref, *rest, final):
    j = pl.program_id(1)
    nj = pl.num_programs(1)
    tn = w_ref.shape[1]
    acc = jnp.dot(ya_ref[...], w_ref[0:CONV_WIDTH, :], preferred_element_type=F32)
    acc = acc + jnp.dot(yb_ref[...], w_ref[CONV_WIDTH:CONV_WIDTH + ATTN_WIDTH, :], preferred_element_type=F32)
    acc = acc + jnp.dot(yc_ref[...], w_ref[CONV_WIDTH + ATTN_WIDTH:, :], preferred_element_type=F32)
    xn = x_ref[...] + acc
    sq = xn * xn
    part = sq[:, 0:LANES]
    for c in range(1, tn // LANES):
        part = part + sq[:, c * LANES:(c + 1) * LANES]

    if final:
        o_ref, ssq_scr = rest
    else:
        xn_ref, xg_ref, r_ref, ssq_scr = rest

    @pl.when(j == 0)
    def _():
        ssq_scr[...] = part

    @pl.when(j > 0)
    def _():
        ssq_scr[...] = ssq_scr[...] + part

    def row_scale():
        ssq = jnp.sum(ssq_scr[...], axis=-1, keepdims=True)
        return lax.rsqrt(ssq * (1.0 / D_MODEL) + NORM_EPS)

    if final:
        for jj in range(D_MODEL // tn):
            @pl.when(j == jj)
            def _(jj=jj):
                o_ref[:, jj * tn:(jj + 1) * tn] = xn

        @pl.when(j == nj - 1)
        def _():
            r = row_scale()
            for jj in range(D_MODEL // tn):
                cols = slice(jj * tn, (jj + 1) * tn)
                o_ref[:, cols] = o_ref[:, cols] * r * g_ref[:, cols]
    else:
        xn_ref[...] = xn
        xg_ref[...] = (xn * g_ref[...]).astype(BF16)

        @pl.when(j == nj - 1)
        def _():
            r_ref[...] = jnp.broadcast_to(row_scale(), r_ref.shape)


def _outproj(ya, yb, yc, w_bf, layer, x2, g, *, final, tm, tn):
    t = x2.shape[0]
    grid = (t // tm, D_MODEL // tn)
    in_specs = [pl.BlockSpec((tm, CONV_WIDTH), lambda i, j: (i, 0)),
                pl.BlockSpec((tm, ATTN_WIDTH), lambda i, j: (i, 0)),
                pl.BlockSpec((tm, POOL_WIDTH), lambda i, j: (i, 0)),
                pl.BlockSpec((None, D_MODEL, tn), lambda i, j: (layer, 0, j)),
                pl.BlockSpec((tm, tn), lambda i, j: (i, j))]
    blocks = [_nbytes((tm, D_MODEL), BF16), _nbytes((D_MODEL, tn), BF16), _nbytes((tm, tn), F32)]
    if final:
        in_specs.append(pl.BlockSpec((1, D_MODEL), lambda i, j: (0, 0)))
        out_specs = pl.BlockSpec((tm, D_MODEL), lambda i, j: (i, 0))
        out_shape = jax.ShapeDtypeStruct((t, D_MODEL), F32)
        blocks.append(_nbytes((tm, D_MODEL), F32))
    else:
        in_specs.append(pl.BlockSpec((1, tn), lambda i, j: (0, j)))
        out_specs = [pl.BlockSpec((tm, tn), lambda i, j: (i, j)),
                     pl.BlockSpec((tm, tn), lambda i, j: (i, j)),
                     pl.BlockSpec((tm, LANES), lambda i, j: (i, 0))]
        out_shape = [jax.ShapeDtypeStruct((t, D_MODEL), F32),
                     jax.ShapeDtypeStruct((t, D_MODEL), BF16),
                     jax.ShapeDtypeStruct((t, LANES), F32)]
        blocks += [_nbytes((tm, tn), F32), _nbytes((tm, tn), BF16)]
    return pl.pallas_call(
        functools.partial(_outproj_kernel, final=final),
        grid=grid,
        in_specs=in_specs,
        out_specs=out_specs,
        out_shape=out_shape,
        scratch_shapes=[pltpu.VMEM((tm, LANES), F32)],
        compiler_params=pltpu.CompilerParams(
            dimension_semantics=("parallel", "arbitrary"),
            vmem_limit_bytes=_vmem_limit(*blocks, scratch=2 * _nbytes((tm, tn), F32))),
        name="outproj_final" if final else "outproj",
    )(ya, yb, yc, w_bf, x2, g.reshape(1, D_MODEL))


def kernel(x, norm_g, w_in, conv_dw, conv_ln_g, conv_ln_b, conv_pw, attn_sinks, pool_w, pool_scale,
           w_out, final_norm_g):
    b, s, d = x.shape
    assert d == D_MODEL and w_in.shape[-1] == IN_WIDTH and s % 512 == 0
    depth = w_in.shape[0]
    t = b * s
    w_in_bf = w_in.astype(BF16)
    w_out_bf = w_out.astype(BF16)
    conv_pw_bf = conv_pw.astype(BF16)
    pool_w_bf = pool_w.astype(BF16)

    x2 = x.reshape(t, d)
    xg, r = _prenorm(x2, norm_g[0])
    for l in range(depth):
        proj3 = _inproj(xg, w_in_bf, l, r).reshape(b, s, IN_WIDTH)
        ya = _conv_branch(proj3, conv_dw[l], conv_ln_g[l], conv_ln_b[l], conv_pw_bf[l]).reshape(t, CONV_WIDTH)
        yb = _attn_branch(proj3, attn_sinks[l]).reshape(t, ATTN_WIDTH)
        yc = _pool_branch(proj3, pool_w_bf[l], pool_scale[l]).reshape(t, POOL_WIDTH)
        if l + 1 < depth:
            x2, xg, r = _outproj(ya, yb, yc, w_out_bf, l, x2, norm_g[l + 1], final=False, tm=1024, tn=512)
        else:
            x2 = _outproj(ya, yb, yc, w_out_bf, l, x2, final_norm_g, final=True, tm=512, tn=512)
    return x2.reshape(b, s, d)
```

```python
import functools

import jax
import jax.numpy as jnp
from jax import lax
from jax.experimental import pallas as pl
from jax.experimental.pallas import tpu as pltpu

F32 = jnp.float32
BF16 = jnp.bfloat16

D_MODEL = 4096
CONV_WIDTH = 1024
ATTN_WIDTH = 2048
POOL_WIDTH = 1024
HEAD_DIM = 64
N_Q_HEADS = 32
N_KV_HEADS = 4
Q_PER_KV = N_Q_HEADS // N_KV_HEADS
KV_WIDTH = N_KV_HEADS * HEAD_DIM
WINDOW = 128
BLOCK = 128
CONV_KERNEL = 31
POOL_WINDOWS = (2, 4, 8, 16)
POOL_GROUP = 256
NORM_EPS = 1e-5
LN_EPS = 1e-5
IN_WIDTH = 9728
LOG2E = 1.4426950408889634

LANES = 128
SUBLANES = 8
V7X_VMEM_BYTES = 64 * 1024 * 1024
VMEM_HEADROOM_BYTES = 6 * 1024 * 1024

PROJ_TILE = 512
PROJ_SRC_TILE = (6, 7, 8, 9, 11, 12, 13, 14, 0, 1, 2, 3, 4, 5, 15, 16, 17, 18, 10)
Q_BLK2048, ZB_BLK2048 = 0, 1
A_BLK1024, B_BLK1024, ZA_BLK1024, UC_BLK1024, ZC_BLK1024 = 4, 5, 6, 7, 8
KV_BLK512 = 18

CONV_HALO = 32
POOL_HALO = 16
CONV_STEPS = 64
NORM_ROWS = 32
NORM_UNROLL = 4


def _vmem_limit(*block_bytes, scratch=0):
    need = 2 * sum(block_bytes) + scratch + VMEM_HEADROOM_BYTES
    return min(need, V7X_VMEM_BYTES - 2 * 1024 * 1024)


def _nbytes(shape, dtype):
    n = 1
    for s in shape:
        n *= s
    return n * jnp.dtype(dtype).itemsize


def _silu(z):
    return z * jax.nn.sigmoid(z)


def _prenorm_kernel(x_ref, g_ref, xg_ref, r_ref):
    x = x_ref[...]
    ssq = jnp.sum(x * x, axis=-1, keepdims=True)
    r = lax.rsqrt(ssq * (1.0 / D_MODEL) + NORM_EPS)
    xg_ref[...] = (x * g_ref[...]).astype(BF16)
    r_ref[...] = jnp.broadcast_to(r, r_ref.shape)


def _prenorm(x2, g, *, rows=256):
    t = x2.shape[0]
    return pl.pallas_call(
        _prenorm_kernel,
        grid=(t // rows,),
        in_specs=[pl.BlockSpec((rows, D_MODEL), lambda i: (i, 0)),
                  pl.BlockSpec((1, D_MODEL), lambda i: (0, 0))],
        out_specs=[pl.BlockSpec((rows, D_MODEL), lambda i: (i, 0)),
                   pl.BlockSpec((rows, LANES), lambda i: (i, 0))],
        out_shape=[jax.ShapeDtypeStruct((t, D_MODEL), BF16),
                   jax.ShapeDtypeStruct((t, LANES), F32)],
        compiler_params=pltpu.CompilerParams(
            dimension_semantics=("parallel",),
            vmem_limit_bytes=_vmem_limit(_nbytes((rows, D_MODEL), F32), _nbytes((rows, D_MODEL), BF16))),
        name="prenorm",
    )(x2, g.reshape(1, D_MODEL))


def _inproj_kernel(src_ref, xg_ref, w_ref, r_ref, *rest):
    del src_ref
    n_cast = (len(rest) - 1) // 2
    cast_in, o_ref, cast_out = rest[:n_cast], rest[n_cast], rest[n_cast + 1:]
    acc = jnp.dot(xg_ref[...], w_ref[...], preferred_element_type=F32)
    r = jnp.tile(r_ref[...], (1, acc.shape[1] // LANES))
    o_ref[...] = (acc * r).astype(o_ref.dtype)
    for src_blk, dst_blk in zip(cast_in, cast_out):
        dst_blk[...] = src_blk[...].astype(dst_blk.dtype)


def _inproj(xg, w_bf, layer, r, cast_jobs=(), *, tm=2048):
    t = xg.shape[0]
    n_tiles = IN_WIDTH // PROJ_TILE
    n_rows = t // tm
    src = jnp.asarray(PROJ_SRC_TILE, dtype=jnp.int32)
    in_specs = [pl.BlockSpec((tm, D_MODEL), lambda i, j, s: (i, 0)),
                pl.BlockSpec((None, D_MODEL, PROJ_TILE), lambda i, j, s: (layer, 0, s[j])),
                pl.BlockSpec((tm, LANES), lambda i, j, s: (i, 0))]
    out_specs = [pl.BlockSpec((tm, PROJ_TILE), lambda i, j, s: (i, j))]
    out_shape = [jax.ShapeDtypeStruct((t, IN_WIDTH), BF16)]
    cast_bytes = 0
    for w_f32, w_layer in cast_jobs:
        _, k_dim, cols = w_f32.shape
        blk_rows = k_dim // n_rows
        blk_cols = PROJ_TILE if cols % (PROJ_TILE * n_tiles) == 0 else cols // 16
        last = cols // blk_cols - 1
        in_specs.append(pl.BlockSpec((None, blk_rows, blk_cols),
                                     lambda i, j, s, w_layer=w_layer, last=last: (w_layer, i, jnp.minimum(j, last))))
        out_specs.append(pl.BlockSpec((None, blk_rows, blk_cols),
                                      lambda i, j, s, last=last: (0, i, jnp.minimum(j, last))))
        out_shape.append(jax.ShapeDtypeStruct((1, k_dim, cols), BF16))
        cast_bytes += _nbytes((blk_rows, blk_cols), F32) + _nbytes((blk_rows, blk_cols), BF16)
    grid_spec = pltpu.PrefetchScalarGridSpec(
        num_scalar_prefetch=1, grid=(n_rows, n_tiles), in_specs=in_specs, out_specs=out_specs)
    outs = pl.pallas_call(
        _inproj_kernel,
        grid_spec=grid_spec,
        out_shape=out_shape,
        compiler_params=pltpu.CompilerParams(
            dimension_semantics=("arbitrary", "arbitrary"),
            vmem_limit_bytes=_vmem_limit(_nbytes((tm, D_MODEL), BF16), _nbytes((D_MODEL, PROJ_TILE), BF16),
                                         _nbytes((tm, LANES), F32), _nbytes((tm, PROJ_TILE), BF16), cast_bytes)),
        name="inproj",
    )(src, xg, w_bf, r, *[w for w, _ in cast_jobs])
    return outs[0], outs[1:]


def _slab_pitch(rows):
    p = -(-rows // SUBLANES)
    if p % 2 == 0:
        p += 1
    return p * SUBLANES


def _conv_kernel(a_ref, b_ref, za_ref, ah_ref, bh_ref, dw_ref, lng_ref, lnb_ref, pw_ref, o_ref,
                 u_scr, y_scr, s_scr, *, u_pitch, y_pitch):
    i = pl.program_id(1)
    ts = a_ref.shape[1]
    n_col = CONV_WIDTH // LANES
    for c in range(n_col):
        lanes = slice(c * LANES, (c + 1) * LANES)
        uh = ah_ref[0, :, lanes].astype(F32) * jax.nn.sigmoid(bh_ref[0, :, lanes].astype(F32))
        u_scr[c * u_pitch:c * u_pitch + CONV_HALO, :] = jnp.where(i > 0, uh, 0.0)
        u_scr[c * u_pitch + CONV_HALO:c * u_pitch + CONV_HALO + ts, :] = (
            a_ref[0, :, lanes].astype(F32) * jax.nn.sigmoid(b_ref[0, :, lanes].astype(F32)))

    first_tap = CONV_HALO - (CONV_KERNEL - 1)

    def conv_chunk(ci, carry):
        t0 = ci * CONV_STEPS
        taps_in = [u_scr[pl.ds(t0 + first_tap + k, n_col, stride=u_pitch), :]
                   for k in range(CONV_STEPS + CONV_KERNEL - 1)]
        for r in range(CONV_STEPS):
            acc = dw_ref[0] * taps_in[r]
            for j in range(1, CONV_KERNEL):
                acc = acc + dw_ref[j] * taps_in[r + j]
            y_scr[pl.ds(t0 + r, n_col, stride=y_pitch), :] = acc
        return carry

    lax.fori_loop(0, ts // CONV_STEPS, conv_chunk, 0)

    lng = lng_ref[...]
    lnb = lnb_ref[...]

    def norm_chunk(ci, carry):
        r0 = pl.multiple_of(ci * NORM_ROWS, NORM_ROWS)
        y = jnp.concatenate([y_scr[pl.ds(c * y_pitch + r0, NORM_ROWS), :] for c in range(n_col)], axis=1)
        mu = jnp.mean(y, axis=-1, keepdims=True)
        d = y - mu
        var = jnp.mean(d * d, axis=-1, keepdims=True)
        yn = d * lax.rsqrt(var + LN_EPS) * lng + lnb
        s_scr[pl.ds(r0, NORM_ROWS), :] = _silu(yn).astype(BF16)
        return carry

    lax.fori_loop(0, ts // NORM_ROWS, norm_chunk, 0, unroll=NORM_UNROLL)
    y = jnp.dot(s_scr[...], pw_ref[...], preferred_element_type=F32)
    o_ref[0] = (y * _silu(za_ref[0].astype(F32))).astype(o_ref.dtype)


def _conv_branch(proj3, dw, ln_g, ln_b, pw_bf, *, ts=512):
    b, s, _ = proj3.shape
    halo_per_blk = ts // CONV_HALO
    n_col = CONV_WIDTH // LANES
    u_pitch = _slab_pitch(CONV_HALO + ts)
    y_pitch = _slab_pitch(ts)
    cur = lambda blk: pl.BlockSpec((1, ts, CONV_WIDTH), lambda bi, i: (bi, i, blk))
    halo = lambda blk: pl.BlockSpec((1, CONV_HALO, CONV_WIDTH),
                                    lambda bi, i: (bi, jnp.maximum(i * halo_per_blk - 1, 0), blk))
    full = lambda shape: pl.BlockSpec(shape, lambda bi, i: (0,) * len(shape))
    scratch = [pltpu.VMEM((n_col * u_pitch, LANES), F32),
               pltpu.VMEM((n_col * y_pitch, LANES), F32),
               pltpu.VMEM((ts, CONV_WIDTH), BF16)]
    return pl.pallas_call(
        functools.partial(_conv_kernel, u_pitch=u_pitch, y_pitch=y_pitch),
        grid=(b, s // ts),
        in_specs=[cur(A_BLK1024), cur(B_BLK1024), cur(ZA_BLK1024), halo(A_BLK1024), halo(B_BLK1024),
                  full((CONV_KERNEL, n_col, LANES)), full((1, CONV_WIDTH)), full((1, CONV_WIDTH)),
                  full((CONV_WIDTH, CONV_WIDTH))],
        out_specs=pl.BlockSpec((1, ts, CONV_WIDTH), lambda bi, i: (bi, i, 0)),
        out_shape=jax.ShapeDtypeStruct((b, s, CONV_WIDTH), BF16),
        scratch_shapes=scratch,
        compiler_params=pltpu.CompilerParams(
            dimension_semantics=("parallel", "arbitrary"),
            vmem_limit_bytes=_vmem_limit(4 * _nbytes((ts, CONV_WIDTH), BF16),
                                         2 * _nbytes((CONV_HALO, CONV_WIDTH), BF16),
                                         _nbytes((CONV_WIDTH, CONV_WIDTH), BF16),
                                         _nbytes((CONV_KERNEL + 2, CONV_WIDTH), F32),
                                         scratch=sum(_nbytes(sc.shape, sc.dtype) for sc in scratch)
                                         + 2 * _nbytes((ts, CONV_WIDTH), F32))),
        name="conv_branch",
    )(proj3, proj3, proj3, proj3, proj3, dw.reshape(CONV_KERNEL, n_col, LANES),
      ln_g.reshape(1, -1), ln_b.reshape(1, -1), pw_bf)


def _attn_kernel(sink_ref, q_ref, zb_ref, kvc_ref, kvp_ref, o_ref):
    i = pl.program_id(1)
    tq = q_ref.shape[1]
    row = lax.broadcasted_iota(jnp.int32, (BLOCK, 2 * BLOCK), 0)
    col = lax.broadcasted_iota(jnp.int32, (BLOCK, 2 * BLOCK), 1)
    rel = row + BLOCK - col
    band = (rel >= 0) & (rel <= WINDOW)
    low = lax.broadcasted_iota(jnp.int32, (2 * BLOCK, LANES), 1) < HEAD_DIM
    low_q = lax.broadcasted_iota(jnp.int32, (BLOCK, LANES), 1) < HEAD_DIM
    ones_top = jnp.where(low, 1.0, 0.0).astype(BF16)
    ones_bot = jnp.where(low, 0.0, 1.0).astype(BF16)
    v_right = jnp.concatenate([ones_top, ones_bot], axis=0)

    for qb in range(tq // BLOCK):
        rows = slice(qb * BLOCK, (qb + 1) * BLOCK)
        if qb == 0:
            kv_prev = kvp_ref[0]
            valid = band & (col >= jnp.where(i == 0, BLOCK, 0))
        else:
            kv_prev = kvc_ref[0, (qb - 1) * BLOCK:qb * BLOCK, :]
            valid = band
        bias = jnp.where(valid, 0.0, -jnp.inf).astype(F32)
        bias2 = jnp.concatenate([bias, bias], axis=1)
        kv =jnp.concatenate([kv_prev, kvc_ref[0, rows, :]], axis=0).astype(F32)

        for h in range(N_KV_HEADS):
            def pair_layout(tile):
                swapped = pltpu.roll(tile, HEAD_DIM, axis=1)
                lo_src, hi_src = (tile, swapped) if h % 2 == 0 else (swapped, tile)
                return jnp.concatenate([jnp.where(low, lo_src, 0.0), jnp.where(low, 0.0, hi_src)], axis=0)

            k_tile = kv[:, (h // 2) * LANES:(h // 2 + 1) * LANES]
            v_tile = kv[:, KV_WIDTH + (h // 2) * LANES:KV_WIDTH + (h // 2 + 1) * LANES]
            k_bd = (pair_layout(k_tile) * (HEAD_DIM ** -0.5 * LOG2E)).astype(BF16)
            v_bd = jnp.concatenate([pair_layout(v_tile).astype(BF16), v_right], axis=1)

            for p in range(Q_PER_KV // 2):
                head0 = h * Q_PER_KV + 2 * p
                lanes = slice(head0 * HEAD_DIM, (head0 + 2) * HEAD_DIM)
                s = lax.dot_general(q_ref[0, rows, lanes], k_bd, (((1,), (1,)), ((), ())),
                                    preferred_element_type=F32) + bias2
                m0 = jnp.max(s[:, :2 * BLOCK], axis=-1, keepdims=True)
                m1 = jnp.max(s[:, 2 * BLOCK:], axis=-1, keepdims=True)
                pexp = jnp.concatenate([jnp.exp2(s[:, :2 * BLOCK] - m0), jnp.exp2(s[:, 2 * BLOCK:] - m1)],
                                       axis=1).astype(BF16)
                o_ext = jnp.dot(pexp, v_bd, preferred_element_type=F32)
                sink2 = jnp.where(low_q[0:1], sink_ref[head0] * LOG2E, sink_ref[head0 + 1] * LOG2E)
                sink_term = jnp.exp2(sink2 - jnp.where(low_q, m0, m1))
                attn = o_ext[:, :LANES] / (o_ext[:, LANES:] + sink_term)
                gate = _silu(zb_ref[0, rows, lanes].astype(F32))
                o_ref[0, rows, lanes] = (attn * gate).astype(o_ref.dtype)


def _attn_branch(proj3, sinks, *, tq=256):
    b, s, _ = proj3.shape
    blocks_per_step = tq // BLOCK
    return pl.pallas_call(
        _attn_kernel,
        grid=(b, s // tq),
        in_specs=[pl.BlockSpec(memory_space=pltpu.SMEM),
                  pl.BlockSpec((1, tq, ATTN_WIDTH), lambda bi, i: (bi, i, Q_BLK2048)),
                  pl.BlockSpec((1, tq, ATTN_WIDTH), lambda bi, i: (bi, i, ZB_BLK2048)),
                  pl.BlockSpec((1, tq, 2 * KV_WIDTH), lambda bi, i: (bi, i, KV_BLK512)),
                  pl.BlockSpec((1, BLOCK, 2 * KV_WIDTH),
                               lambda bi, i: (bi, jnp.maximum(i * blocks_per_step - 1, 0), KV_BLK512))],
        out_specs=pl.BlockSpec((1, tq, ATTN_WIDTH), lambda bi, i: (bi, i, 0)),
        out_shape=jax.ShapeDtypeStruct((b, s, ATTN_WIDTH), BF16),
        compiler_params=pltpu.CompilerParams(
            dimension_semantics=("parallel", "arbitrary"),
            vmem_limit_bytes=_vmem_limit(3 * _nbytes((tq, ATTN_WIDTH), BF16),
                                         2 * _nbytes((tq, 2 * KV_WIDTH), BF16),
                                         scratch=8 * 1024 * 1024)),
        name="attn_branch",
    )(sinks, proj3, proj3, proj3, proj3)


def _pool_kernel(u_ref, z_ref, uh_ref, w_ref, sc_ref, o_ref, u_scr):
    i = pl.program_id(1)
    ts = u_ref.shape[1]
    u_scr[0:POOL_HALO, :] = jnp.where(i > 0, uh_ref[0], jnp.zeros_like(uh_ref[0]))
    u_scr[POOL_HALO:POOL_HALO + ts, :] = u_ref[0]
    k_rows = BLOCK + POOL_HALO
    rel = (lax.broadcasted_iota(jnp.int32, (BLOCK, k_rows), 0) + POOL_HALO
           - lax.broadcasted_iota(jnp.int32, (BLOCK, k_rows), 1))
    pos0 = lax.broadcasted_iota(jnp.int32, (BLOCK, POOL_GROUP), 0) + (i * ts + 1)
    for g, win in enumerate(POOL_WINDOWS):
        cols = slice(g * POOL_GROUP, (g + 1) * POOL_GROUP)
        band = jnp.where((rel >= 0) & (rel < win), 1.0, 0.0).astype(BF16)
        mixed = []
        for rb in range(ts // BLOCK):
            tot = jnp.dot(band, u_scr[rb * BLOCK:rb * BLOCK + k_rows, cols], preferred_element_type=F32)
            cur = u_ref[0, rb * BLOCK:(rb + 1) * BLOCK, cols].astype(F32)
            cnt = jnp.minimum((pos0 + rb * BLOCK).astype(F32), float(win))
            mixed.append((tot / cnt - cur).astype(BF16))
        y = jnp.dot(jnp.concatenate(mixed, axis=0), w_ref[g], preferred_element_type=F32)
        gate = _silu(z_ref[0, :, cols].astype(F32))
        o_ref[0, :, cols] = (y * sc_ref[:, cols] * gate).astype(o_ref.dtype)


def _pool_branch(proj3, pool_w_bf, pool_scale, *, ts=512):
    b, s, _ = proj3.shape
    halo_per_blk = ts // POOL_HALO
    return pl.pallas_call(
        _pool_kernel,
        grid=(b, s // ts),
        in_specs=[pl.BlockSpec((1, ts, POOL_WIDTH), lambda bi, i: (bi, i, UC_BLK1024)),
                  pl.BlockSpec((1, ts, POOL_WIDTH), lambda bi, i: (bi, i, ZC_BLK1024)),
                  pl.BlockSpec((1, POOL_HALO, POOL_WIDTH),
                               lambda bi, i: (bi, jnp.maximum(i * halo_per_blk - 1, 0), UC_BLK1024)),
                  pl.BlockSpec((len(POOL_WINDOWS), POOL_GROUP, POOL_GROUP), lambda bi, i: (0, 0, 0)),
                  pl.BlockSpec((1, POOL_WIDTH), lambda bi, i: (0, 0))],
        out_specs=pl.BlockSpec((1, ts, POOL_WIDTH), lambda bi, i: (bi, i, 0)),
        out_shape=jax.ShapeDtypeStruct((b, s, POOL_WIDTH), BF16),
        scratch_shapes=[pltpu.VMEM((POOL_HALO + ts, POOL_WIDTH), BF16)],
        compiler_params=pltpu.CompilerParams(
            dimension_semantics=("parallel", "arbitrary"),
            vmem_limit_bytes=_vmem_limit(3 * _nbytes((ts, POOL_WIDTH), BF16),
                                         scratch=_nbytes((POOL_HALO + ts, POOL_WIDTH), BF16)
                                         + 4 * _nbytes((ts, POOL_WIDTH), F32))),
        name="pool_branch",
    )(proj3, proj3, proj3, pool_w_bf, pool_scale.reshape(1, -1))


def _outproj_kernel(ya_ref, yb_ref, yc_ref, w_ref, x_ref, g_ref, *rest, final):
    j = pl.program_id(1)
    nj = pl.num_programs(1)
    tn = w_ref.shape[1]
    acc = jnp.dot(ya_ref[...], w_ref[0:CONV_WIDTH, :], preferred_element_type=F32)
    acc = acc + jnp.dot(yb_ref[...], w_ref[CONV_WIDTH:CONV_WIDTH + ATTN_WIDTH, :], preferred_element_type=F32)
    acc = acc + jnp.dot(yc_ref[...], w_ref[CONV_WIDTH + ATTN_WIDTH:, :], preferred_element_type=F32)
    xn = x_ref[...] + acc
    sq = xn * xn
    part = sq[:, 0:LANES]
    for c in range(1, tn // LANES):
        part = part + sq[:, c * LANES:(c + 1) * LANES]

    if final:
        o_ref, ssq_scr = rest
    else:
        xn_ref, xg_ref, r_ref, ssq_scr = rest

    @pl.when(j == 0)
    def _():
        ssq_scr[...] = part

    @pl.when(j > 0)
    def _():
        ssq_scr[...] = ssq_scr[...] + part

    def row_scale():
        ssq = jnp.sum(ssq_scr[...], axis=-1, keepdims=True)
        return lax.rsqrt(ssq * (1.0 / D_MODEL) + NORM_EPS)

    if final:
        for jj in range(D_MODEL // tn):
            @pl.when(j == jj)
            def _(jj=jj):
                o_ref[:, jj * tn:(jj + 1) * tn] = xn

        @pl.when(j == nj - 1)
        def _():
            r = row_scale()
            for jj in range(D_MODEL // tn):
                cols = slice(jj * tn, (jj + 1) * tn)
                o_ref[:, cols] = o_ref[:, cols] * r * g_ref[:, cols]
    else:
        xn_ref[...] = xn
        xg_ref[...] = (xn * g_ref[...]).astype(BF16)

        @pl.when(j == nj - 1)
        def _():
            r_ref[...] = jnp.broadcast_to(row_scale(), r_ref.shape)


def _outproj(ya, yb, yc, w_bf, layer, x2, g, *, final, tm, tn):
    t = x2.shape[0]
    grid = (t // tm, D_MODEL // tn)
    in_specs = [pl.BlockSpec((tm, CONV_WIDTH), lambda i, j: (i, 0)),
                pl.BlockSpec((tm, ATTN_WIDTH), lambda i, j: (i, 0)),
                pl.BlockSpec((tm, POOL_WIDTH), lambda i, j: (i, 0)),
                pl.BlockSpec((None, D_MODEL, tn), lambda i, j: (layer, 0, j)),
                pl.BlockSpec((tm, tn), lambda i, j: (i, j))]
    blocks = [_nbytes((tm, D_MODEL), BF16), _nbytes((D_MODEL, tn), BF16), _nbytes((tm, tn), F32)]
    if final:
        in_specs.append(pl.BlockSpec((1, D_MODEL), lambda i, j: (0, 0)))
        out_specs = pl.BlockSpec((tm, D_MODEL), lambda i, j: (i, 0))
        out_shape = jax.ShapeDtypeStruct((t, D_MODEL), F32)
        blocks.append(_nbytes((tm, D_MODEL), F32))
    else:
        in_specs.append(pl.BlockSpec((1, tn), lambda i, j: (0, j)))
        out_specs = [pl.BlockSpec((tm, tn), lambda i, j: (i, j)),
                     pl.BlockSpec((tm, tn), lambda i, j: (i, j)),
                     pl.BlockSpec((tm, LANES), lambda i, j: (i, 0))]
        out_shape = [jax.ShapeDtypeStruct((t, D_MODEL), F32),
                     jax.ShapeDtypeStruct((t, D_MODEL), BF16),
                     jax.ShapeDtypeStruct((t, LANES), F32)]
        blocks += [_nbytes((tm, tn), F32), _nbytes((tm, tn), BF16)]
    return pl.pallas_call(
        functools.partial(_outproj_kernel, final=final),
        grid=grid,
        in_specs=in_specs,
        out_specs=out_specs,
        out_shape=out_shape,
        scratch_shapes=[pltpu.VMEM((tm, LANES), F32)],
        compiler_params=pltpu.CompilerParams(
            dimension_semantics=("parallel", "arbitrary"),
            vmem_limit_bytes=_vmem_limit(*blocks, scratch=2 * _nbytes((tm, tn), F32))),
        name="outproj_final" if final else "outproj",
    )(ya, yb, yc, w_bf, x2, g.reshape(1, D_MODEL))


def kernel(x, norm_g, w_in, conv_dw, conv_ln_g, conv_ln_b, conv_pw, attn_sinks, pool_w, pool_scale,
           w_out, final_norm_g):
    b, s, d = x.shape
    assert d == D_MODEL and w_in.shape[-1] == IN_WIDTH and s % 512 == 0
    depth = w_in.shape[0]
    t = b * s
    conv_pw_bf = conv_pw.astype(BF16)
    pool_w_bf = pool_w.astype(BF16)
    w_in_bf = [w_in[0:1].astype(BF16)]
    cast_jobs = [(w_in, l) for l in range(1, depth)] + [(w_out, l) for l in range(depth)]

    x2 = x.reshape(t, d)
    xg, r = _prenorm(x2, norm_g[0])
    for l in range(depth):
        proj, casted = _inproj(xg, w_in_bf[l], 0, r, cast_jobs if l == 0 else ())
        if l == 0:
            w_in_bf += list(casted[:depth - 1])
            w_out_bf = list(casted[depth - 1:])
        proj3 = proj.reshape(b, s, IN_WIDTH)
        ya = _conv_branch(proj3, conv_dw[l], conv_ln_g[l], conv_ln_b[l], conv_pw_bf[l]).reshape(t, CONV_WIDTH)
        yb = _attn_branch(proj3, attn_sinks[l]).reshape(t, ATTN_WIDTH)
        yc = _pool_branch(proj3, pool_w_bf[l], pool_scale[l]).reshape(t, POOL_WIDTH)
        if l + 1 < depth:
            x2, xg, r = _outproj(ya, yb, yc, w_out_bf[l], 0, x2, norm_g[l + 1], final=False, tm=1024, tn=512)
        else:
            x2 = _outproj(ya, yb, yc, w_out_bf[l], 0, x2, final_norm_g, final=True, tm=1024, tn=256)
    return x2.reshape(b, s, d)
```

```python
import functools

import jax
import jax.numpy as jnp
from jax import lax
from jax.experimental import pallas as pl
from jax.experimental.pallas import tpu as pltpu

F32 = jnp.float32
BF16 = jnp.bfloat16

D_MODEL = 4096
CONV_WIDTH = 1024
ATTN_WIDTH = 2048
POOL_WIDTH = 1024
HEAD_DIM = 64
N_Q_HEADS = 32
N_KV_HEADS = 4
Q_PER_KV = N_Q_HEADS // N_KV_HEADS
KV_WIDTH = N_KV_HEADS * HEAD_DIM
WINDOW = 128
BLOCK = 128
CONV_KERNEL = 31
POOL_WINDOWS = (2, 4, 8, 16)
POOL_GROUP = 256
NORM_EPS = 1e-5
LN_EPS = 1e-5
IN_WIDTH = 9728
LOG2E = 1.4426950408889634

LANES = 128
SUBLANES = 8
V7X_VMEM_BYTES = 64 * 1024 * 1024
VMEM_HEADROOM_BYTES = 6 * 1024 * 1024

PROJ_TILE = 512
PROJ_SRC_TILE = (6, 7, 8, 9, 11, 12, 13, 14, 0, 1, 2, 3, 4, 5, 15, 16, 17, 18, 10)
Q_BLK2048, ZB_BLK2048 = 0, 1
A_BLK1024, B_BLK1024, ZA_BLK1024, UC_BLK1024, ZC_BLK1024 = 4, 5, 6, 7, 8
KV_BLK512 = 18

CONV_HALO = 32
POOL_HALO = 16
CONV_STEPS = 64
NORM_ROWS = 32
NORM_UNROLL = 4


def _vmem_limit(*block_bytes, scratch=0):
    need = 2 * sum(block_bytes) + scratch + VMEM_HEADROOM_BYTES
    return min(need, V7X_VMEM_BYTES - 2 * 1024 * 1024)


def _nbytes(shape, dtype):
    n = 1
    for s in shape:
        n *= s
    return n * jnp.dtype(dtype).itemsize


def _silu(z):
    return z * jax.nn.sigmoid(z)


def _prenorm_kernel(x_ref, g_ref, xg_ref, r_ref):
    x = x_ref[...]
    ssq = jnp.sum(x * x, axis=-1, keepdims=True)
    r = lax.rsqrt(ssq * (1.0 / D_MODEL) + NORM_EPS)
    xg_ref[...] = (x * g_ref[...]).astype(BF16)
    r_ref[...] = jnp.broadcast_to(r, r_ref.shape)


def _prenorm(x2, g, *, rows=256):
    t = x2.shape[0]
    return pl.pallas_call(
        _prenorm_kernel,
        grid=(t // rows,),
        in_specs=[pl.BlockSpec((rows, D_MODEL), lambda i: (i, 0)),
                  pl.BlockSpec((1, D_MODEL), lambda i: (0, 0))],
        out_specs=[pl.BlockSpec((rows, D_MODEL), lambda i: (i, 0)),
                   pl.BlockSpec((rows, LANES), lambda i: (i, 0))],
        out_shape=[jax.ShapeDtypeStruct((t, D_MODEL), BF16),
                   jax.ShapeDtypeStruct((t, LANES), F32)],
        compiler_params=pltpu.CompilerParams(
            dimension_semantics=("parallel",),
            vmem_limit_bytes=_vmem_limit(_nbytes((rows, D_MODEL), F32), _nbytes((rows, D_MODEL), BF16))),
        name="prenorm",
    )(x2, g.reshape(1, D_MODEL))


def _inproj_kernel(src_ref, xg_ref, w_ref, r_ref, *rest):
    del src_ref
    n_cast = (len(rest) - 1) // 2
    cast_in, o_ref, cast_out = rest[:n_cast], rest[n_cast], rest[n_cast + 1:]
    acc = jnp.dot(xg_ref[...], w_ref[...], preferred_element_type=F32)
    r = jnp.tile(r_ref[...], (1, acc.shape[1] // LANES))
    o_ref[...] = (acc * r).astype(o_ref.dtype)
    for src_blk, dst_blk in zip(cast_in, cast_out):
        dst_blk[...] = src_blk[...].astype(dst_blk.dtype)


def _inproj(xg, w_bf, layer, r, cast_jobs=(), *, tm=2048):
    t = xg.shape[0]
    n_tiles = IN_WIDTH // PROJ_TILE
    n_rows = t // tm
    src = jnp.asarray(PROJ_SRC_TILE, dtype=jnp.int32)
    in_specs = [pl.BlockSpec((tm, D_MODEL), lambda i, j, s: (i, 0)),
                pl.BlockSpec((None, D_MODEL, PROJ_TILE), lambda i, j, s: (layer, 0, s[j])),
                pl.BlockSpec((tm, LANES), lambda i, j, s: (i, 0))]
    out_specs = [pl.BlockSpec((tm, PROJ_TILE), lambda i, j, s: (i, j))]
    out_shape = [jax.ShapeDtypeStruct((t, IN_WIDTH), BF16)]
    cast_bytes = 0
    for w_f32, w_layer in cast_jobs:
        _, k_dim, cols = w_f32.shape
        blk_rows = k_dim // n_rows
        blk_cols = PROJ_TILE if cols % (PROJ_TILE * n_tiles) == 0 else cols // 16
        last = cols // blk_cols - 1
        in_specs.append(pl.BlockSpec((None, blk_rows, blk_cols),
                                     lambda i, j, s, w_layer=w_layer, last=last: (w_layer, i, jnp.minimum(j, last))))
        out_specs.append(pl.BlockSpec((None, blk_rows, blk_cols),
                                      lambda i, j, s, last=last: (0, i, jnp.minimum(j, last))))
        out_shape.append(jax.ShapeDtypeStruct((1, k_dim, cols), BF16))
        cast_bytes += _nbytes((blk_rows, blk_cols), F32) + _nbytes((blk_rows, blk_cols), BF16)
    grid_spec = pltpu.PrefetchScalarGridSpec(
        num_scalar_prefetch=1, grid=(n_rows, n_tiles), in_specs=in_specs, out_specs=out_specs)
    outs = pl.pallas_call(
        _inproj_kernel,
        grid_spec=grid_spec,
        out_shape=out_shape,
        compiler_params=pltpu.CompilerParams(
            dimension_semantics=("arbitrary", "arbitrary"),
            vmem_limit_bytes=_vmem_limit(_nbytes((tm, D_MODEL), BF16), _nbytes((D_MODEL, PROJ_TILE), BF16),
                                         _nbytes((tm, LANES), F32), _nbytes((tm, PROJ_TILE), BF16), cast_bytes)),
        name="inproj",
    )(src, xg, w_bf, r, *[w for w, _ in cast_jobs])
    return outs[0], outs[1:]


def _slab_pitch(rows):
    p = -(-rows // SUBLANES)
    if p % 2 == 0:
        p += 1
    return p * SUBLANES


def _conv_kernel(a_ref, b_ref, za_ref, ah_ref, bh_ref, dw_ref, lng_ref, lnb_ref, pw_ref, o_ref,
                 u_scr, y_scr, s_scr, *, u_pitch, y_pitch):
    i = pl.program_id(1)
    ts = a_ref.shape[1]
    n_col = CONV_WIDTH // LANES
    for c in range(n_col):
        lanes = slice(c * LANES, (c + 1) * LANES)
        uh = ah_ref[0, :, lanes].astype(F32) * jax.nn.sigmoid(bh_ref[0, :, lanes].astype(F32))
        u_scr[c * u_pitch:c * u_pitch + CONV_HALO, :] = jnp.where(i > 0, uh, 0.0)
        u_scr[c * u_pitch + CONV_HALO:c * u_pitch + CONV_HALO + ts, :] = (
            a_ref[0, :, lanes].astype(F32) * jax.nn.sigmoid(b_ref[0, :, lanes].astype(F32)))

    first_tap = CONV_HALO - (CONV_KERNEL - 1)

    def conv_chunk(ci, carry):
        t0 = ci * CONV_STEPS
        taps_in = [u_scr[pl.ds(t0 + first_tap + k, n_col, stride=u_pitch), :]
                   for k in range(CONV_STEPS + CONV_KERNEL - 1)]
        for r in range(CONV_STEPS):
            acc = dw_ref[0] * taps_in[r]
            for j in range(1, CONV_KERNEL):
                acc = acc + dw_ref[j] * taps_in[r + j]
            y_scr[pl.ds(t0 + r, n_col, stride=y_pitch), :] = acc
        return carry

    lax.fori_loop(0, ts // CONV_STEPS, conv_chunk, 0)

    lng = lng_ref[...]
    lnb = lnb_ref[...]

    def norm_chunk(ci, carry):
        r0 = pl.multiple_of(ci * NORM_ROWS, NORM_ROWS)
        y = jnp.concatenate([y_scr[pl.ds(c * y_pitch + r0, NORM_ROWS), :] for c in range(n_col)], axis=1)
        mu = jnp.mean(y, axis=-1, keepdims=True)
        d = y - mu
        var = jnp.mean(d * d, axis=-1, keepdims=True)
        yn = d * lax.rsqrt(var + LN_EPS) * lng + lnb
        s_scr[pl.ds(r0, NORM_ROWS), :] = _silu(yn).astype(BF16)
        return carry

    lax.fori_loop(0, ts // NORM_ROWS, norm_chunk, 0, unroll=NORM_UNROLL)
    y = jnp.dot(s_scr[...], pw_ref[...], preferred_element_type=F32)
    o_ref[0] = (y * _silu(za_ref[0].astype(F32))).astype(o_ref.dtype)


def _conv_branch(proj3, dw, ln_g, ln_b, pw_bf, *, ts=512):
    b, s, _ = proj3.shape
    halo_per_blk = ts // CONV_HALO
    n_col = CONV_WIDTH // LANES
    u_pitch = _slab_pitch(CONV_HALO + ts)
    y_pitch = _slab_pitch(ts)
    cur = lambda blk: pl.BlockSpec((1, ts, CONV_WIDTH), lambda bi, i: (bi, i, blk))
    halo = lambda blk: pl.BlockSpec((1, CONV_HALO, CONV_WIDTH),
                                    lambda bi, i: (bi, jnp.maximum(i * halo_per_blk - 1, 0), blk))
    full = lambda shape: pl.BlockSpec(shape, lambda bi, i: (0,) * len(shape))
    scratch = [pltpu.VMEM((n_col * u_pitch, LANES), F32),
               pltpu.VMEM((n_col * y_pitch, LANES), F32),
               pltpu.VMEM((ts, CONV_WIDTH), BF16)]
    return pl.pallas_call(
        functools.partial(_conv_kernel, u_pitch=u_pitch, y_pitch=y_pitch),
        grid=(b, s // ts),
        in_specs=[cur(A_BLK1024), cur(B_BLK1024), cur(ZA_BLK1024), halo(A_BLK1024), halo(B_BLK1024),
                  full((CONV_KERNEL, n_col, LANES)), full((1, CONV_WIDTH)), full((1, CONV_WIDTH)),
                  full((CONV_WIDTH, CONV_WIDTH))],
        out_specs=pl.BlockSpec((1, ts, CONV_WIDTH), lambda bi, i: (bi, i, 0)),
        out_shape=jax.ShapeDtypeStruct((b, s, CONV_WIDTH), BF16),
        scratch_shapes=scratch,
        compiler_params=pltpu.CompilerParams(
            dimension_semantics=("parallel", "arbitrary"),
            vmem_limit_bytes=_vmem_limit(4 * _nbytes((ts, CONV_WIDTH), BF16),
                                         2 * _nbytes((CONV_HALO, CONV_WIDTH), BF16),
                                         _nbytes((CONV_WIDTH, CONV_WIDTH), BF16),
                                         _nbytes((CONV_KERNEL + 2, CONV_WIDTH), F32),
                                         scratch=sum(_nbytes(sc.shape, sc.dtype) for sc in scratch)
                                         + 2 * _nbytes((ts, CONV_WIDTH), F32))),
        name="conv_branch",
    )(proj3, proj3, proj3, proj3, proj3, dw.reshape(CONV_KERNEL, n_col, LANES),
      ln_g.reshape(1, -1), ln_b.reshape(1, -1), pw_bf)


def _attn_kernel(sink_ref, q_ref, zb_ref, kvc_ref, kvp_ref, o_ref):
    i = pl.program_id(1)
    tq = q_ref.shape[1]
    row = lax.broadcasted_iota(jnp.int32, (BLOCK, 2 * BLOCK), 0)
    col = lax.broadcasted_iota(jnp.int32, (BLOCK, 2 * BLOCK), 1)
    rel = row + BLOCK - col
    band = (rel >= 0) & (rel <= WINDOW)
    low = lax.broadcasted_iota(jnp.int32, (2 * BLOCK, LANES), 1) < HEAD_DIM
    low_q = lax.broadcasted_iota(jnp.int32, (BLOCK, LANES), 1) < HEAD_DIM
    ones_top = jnp.where(low, 1.0, 0.0).astype(BF16)
    ones_bot = jnp.where(low, 0.0, 1.0).astype(BF16)
    v_right = jnp.concatenate([ones_top, ones_bot], axis=0)

    for qb in range(tq // BLOCK):
        rows = slice(qb * BLOCK, (qb + 1) * BLOCK)
        if qb == 0:
            kv_prev = kvp_ref[0]
            valid = band & (col >= jnp.where(i == 0, BLOCK, 0))
        else:
            kv_prev = kvc_ref[0, (qb - 1) * BLOCK:qb * BLOCK, :]
            valid = band
        bias = jnp.where(valid, 0.0, -jnp.inf).astype(F32)
        bias2 = jnp.concatenate([bias, bias], axis=1)
        kv =jnp.concatenate([kv_prev, kvc_ref[0, rows, :]], axis=0).astype(F32)

        for h in range(N_KV_HEADS):
            def pair_layout(tile):
                swapped = pltpu.roll(tile, HEAD_DIM, axis=1)
                lo_src, hi_src = (tile, swapped) if h % 2 == 0 else (swapped, tile)
                return jnp.concatenate([jnp.where(low, lo_src, 0.0), jnp.where(low, 0.0, hi_src)], axis=0)

            k_tile = kv[:, (h // 2) * LANES:(h // 2 + 1) * LANES]
            v_tile = kv[:, KV_WIDTH + (h // 2) * LANES:KV_WIDTH + (h // 2 + 1) * LANES]
            k_bd = (pair_layout(k_tile) * (HEAD_DIM ** -0.5 * LOG2E)).astype(BF16)
            v_bd = jnp.concatenate([pair_layout(v_tile).astype(BF16), v_right], axis=1)

            for p in range(Q_PER_KV // 2):
                head0 = h * Q_PER_KV + 2 * p
                lanes = slice(head0 * HEAD_DIM, (head0 + 2) * HEAD_DIM)
                s = lax.dot_general(q_ref[0, rows, lanes], k_bd, (((1,), (1,)), ((), ())),
                                    preferred_element_type=F32) + bias2
                m0 = jnp.max(s[:, :2 * BLOCK], axis=-1, keepdims=True)
                m1 = jnp.max(s[:, 2 * BLOCK:], axis=-1, keepdims=True)
                pexp = jnp.concatenate([jnp.exp2(s[:, :2 * BLOCK] - m0), jnp.exp2(s[:, 2 * BLOCK:] - m1)],
                                       axis=1).astype(BF16)
                o_ext = jnp.dot(pexp, v_bd, preferred_element_type=F32)
                sink2 = jnp.where(low_q[0:1], sink_ref[head0] * LOG2E, sink_ref[head0 + 1] * LOG2E)
                sink_term = jnp.exp2(sink2 - jnp.where(low_q, m0, m1))
                attn = o_ext[:, :LANES] / (o_ext[:, LANES:] + sink_term)
                gate = _silu(zb_ref[0, rows, lanes].astype(F32))
                o_ref[0, rows, lanes] = (attn * gate).astype(o_ref.dtype)


def _attn_branch(proj3, sinks, *, tq=256):
    b, s, _ = proj3.shape
    blocks_per_step = tq // BLOCK
    return pl.pallas_call(
        _attn_kernel,
        grid=(b, s // tq),
        in_specs=[pl.BlockSpec(memory_space=pltpu.SMEM),
                  pl.BlockSpec((1, tq, ATTN_WIDTH), lambda bi, i: (bi, i, Q_BLK2048)),
                  pl.BlockSpec((1, tq, ATTN_WIDTH), lambda bi, i: (bi, i, ZB_BLK2048)),
                  pl.BlockSpec((1, tq, 2 * KV_WIDTH), lambda bi, i: (bi, i, KV_BLK512)),
                  pl.BlockSpec((1, BLOCK, 2 * KV_WIDTH),
                               lambda bi, i: (bi, jnp.maximum(i * blocks_per_step - 1, 0), KV_BLK512))],
        out_specs=pl.BlockSpec((1, tq, ATTN_WIDTH), lambda bi, i: (bi, i, 0)),
        out_shape=jax.ShapeDtypeStruct((b, s, ATTN_WIDTH), BF16),
        compiler_params=pltpu.CompilerParams(
            dimension_semantics=("parallel", "arbitrary"),
            vmem_limit_bytes=_vmem_limit(3 * _nbytes((tq, ATTN_WIDTH), BF16),
                                         2 * _nbytes((tq, 2 * KV_WIDTH), BF16),
                                         scratch=8 * 1024 * 1024)),
        name="attn_branch",
    )(sinks, proj3, proj3, proj3, proj3)


def _pool_kernel(u_ref, z_ref, uh_ref, w_ref, sc_ref, o_ref, u_scr):
    i = pl.program_id(1)
    ts = u_ref.shape[1]
    u_scr[0:POOL_HALO, :] = jnp.where(i > 0, uh_ref[0], jnp.zeros_like(uh_ref[0]))
    u_scr[POOL_HALO:POOL_HALO + ts, :] = u_ref[0]
    k_rows = BLOCK + POOL_HALO
    rel = (lax.broadcasted_iota(jnp.int32, (BLOCK, k_rows), 0) + POOL_HALO
           - lax.broadcasted_iota(jnp.int32, (BLOCK, k_rows), 1))
    pos0 = lax.broadcasted_iota(jnp.int32, (BLOCK, POOL_GROUP), 0) + (i * ts + 1)
    for g, win in enumerate(POOL_WINDOWS):
        cols = slice(g * POOL_GROUP, (g + 1) * POOL_GROUP)
        band = jnp.where((rel >= 0) & (rel < win), 1.0, 0.0).astype(BF16)
        mixed = []
        for rb in range(ts // BLOCK):
            tot = jnp.dot(band, u_scr[rb * BLOCK:rb * BLOCK + k_rows, cols], preferred_element_type=F32)
            cur = u_ref[0, rb * BLOCK:(rb + 1) * BLOCK, cols].astype(F32)
            cnt = jnp.minimum((pos0 + rb * BLOCK).astype(F32), float(win))
            mixed.append((tot / cnt - cur).astype(BF16))
        y = jnp.dot(jnp.concatenate(mixed, axis=0), w_ref[g], preferred_element_type=F32)
        gate = _silu(z_ref[0, :, cols].astype(F32))
        o_ref[0, :, cols] = (y * sc_ref[:, cols] * gate).astype(o_ref.dtype)


def _pool_branch(proj3, pool_w_bf, pool_scale, *, ts=512):
    b, s, _ = proj3.shape
    halo_per_blk = ts // POOL_HALO
    return pl.pallas_call(
        _pool_kernel,
        grid=(b, s // ts),
        in_specs=[pl.BlockSpec((1, ts, POOL_WIDTH), lambda bi, i: (bi, i, UC_BLK1024)),
                  pl.BlockSpec((1, ts, POOL_WIDTH), lambda bi, i: (bi, i, ZC_BLK1024)),
                  pl.BlockSpec((1, POOL_HALO, POOL_WIDTH),
                               lambda bi, i: (bi, jnp.maximum(i * halo_per_blk - 1, 0), UC_BLK1024)),
                  pl.BlockSpec((len(POOL_WINDOWS), POOL_GROUP, POOL_GROUP), lambda bi, i: (0, 0, 0)),
                  pl.BlockSpec((1, POOL_WIDTH), lambda bi, i: (0, 0))],
        out_specs=pl.BlockSpec((1, ts, POOL_WIDTH), lambda bi, i: (bi, i, 0)),
        out_shape=jax.ShapeDtypeStruct((b, s, POOL_WIDTH), BF16),
        scratch_shapes=[pltpu.VMEM((POOL_HALO + ts, POOL_WIDTH), BF16)],
        compiler_params=pltpu.CompilerParams(
            dimension_semantics=("parallel", "arbitrary"),
            vmem_limit_bytes=_vmem_limit(3 * _nbytes((ts, POOL_WIDTH), BF16),
                                         scratch=_nbytes((POOL_HALO + ts, POOL_WIDTH), BF16)
                                         + 4 * _nbytes((ts, POOL_WIDTH), F32))),
        name="pool_branch",
    )(proj3, proj3, proj3, pool_w_bf, pool_scale.reshape(1, -1))


def _outproj_kernel(ya_ref, yb_ref, yc_ref, w_ref, x_ref, g_ref, *rest, final):
    j = pl.program_id(1)
    tn = w_ref.shape[1]
    nj = D_MODEL // tn
    if final:
        o_ref, ssq_scr, xn_scr, r_scr = rest
    else:
        xn_ref, xg_ref, r_ref, ssq_scr = rest

    def row_scale():
        ssq = jnp.sum(ssq_scr[...], axis=-1, keepdims=True)
        return lax.rsqrt(ssq * (1.0 / D_MODEL) + NORM_EPS)

    def matmul_step():
        acc = jnp.dot(ya_ref[...], w_ref[0:CONV_WIDTH, :], preferred_element_type=F32)
        acc = acc + jnp.dot(yb_ref[...], w_ref[CONV_WIDTH:CONV_WIDTH + ATTN_WIDTH, :],
                            preferred_element_type=F32)
        acc = acc + jnp.dot(yc_ref[...], w_ref[CONV_WIDTH + ATTN_WIDTH:, :], preferred_element_type=F32)
        xn = x_ref[...] + acc
        sq = xn * xn
        part = sq[:, 0:LANES]
        for c in range(1, tn // LANES):
            part = part + sq[:, c * LANES:(c + 1) * LANES]

        @pl.when(j == 0)
        def _():
            ssq_scr[...] = part

        @pl.when(j > 0)
        def _():
            ssq_scr[...] = ssq_scr[...] + part

        if final:
            xn_scr[j] = xn

            @pl.when(j == nj - 1)
            def _():
                r_scr[...] = jnp.broadcast_to(row_scale(), r_scr.shape)
        else:
            xn_ref[...] = xn
            xg_ref[...] = (xn * g_ref[...]).astype(BF16)

            @pl.when(j == nj - 1)
            def _():
                r_ref[...] = jnp.broadcast_to(row_scale(), r_ref.shape)

    if final:
        pl.when(j < nj)(matmul_step)

        @pl.when(j >= nj)
        def _():
            r = jnp.tile(r_scr[...], (1, tn // LANES))
            o_ref[...] = xn_scr[j - nj] * r * g_ref[...]
    else:
        matmul_step()


def _outproj(ya, yb, yc, w_bf, layer, x2, g, *, final, tm, tn):
    t = x2.shape[0]
    nj = D_MODEL // tn
    mm = (lambda j: jnp.minimum(j, nj - 1)) if final else (lambda j: j)
    in_specs = [pl.BlockSpec((tm, CONV_WIDTH), lambda i, j: (i, 0)),
                pl.BlockSpec((tm, ATTN_WIDTH), lambda i, j: (i, 0)),
                pl.BlockSpec((tm, POOL_WIDTH), lambda i, j: (i, 0)),
                pl.BlockSpec((None, D_MODEL, tn), lambda i, j: (layer, 0, mm(j))),
                pl.BlockSpec((tm, tn), lambda i, j: (i, mm(j)))]
    blocks = [_nbytes((tm, D_MODEL), BF16), _nbytes((D_MODEL, tn), BF16), _nbytes((tm, tn), F32)]
    scratch = [pltpu.VMEM((tm, LANES), F32)]
    if final:
        grid = (t // tm, 2 * nj)
        out_tile = lambda i, j: (i, jnp.maximum(j - nj, 0))
        in_specs.append(pl.BlockSpec((1, tn), lambda i, j: (0, jnp.maximum(j - nj, 0))))
        out_specs = pl.BlockSpec((tm, tn), out_tile)
        out_shape = jax.ShapeDtypeStruct((t, D_MODEL), F32)
        blocks.append(_nbytes((tm, tn), F32))
        scratch += [pltpu.VMEM((nj, tm, tn), F32), pltpu.VMEM((tm, LANES), F32)]
    else:
        grid = (t // tm, nj)
        in_specs.append(pl.BlockSpec((1, tn), lambda i, j: (0, j)))
        out_specs = [pl.BlockSpec((tm, tn), lambda i, j: (i, j)),
                     pl.BlockSpec((tm, tn), lambda i, j: (i, j)),
                     pl.BlockSpec((tm, LANES), lambda i, j: (i, 0))]
        out_shape = [jax.ShapeDtypeStruct((t, D_MODEL), F32),
                     jax.ShapeDtypeStruct((t, D_MODEL), BF16),
                     jax.ShapeDtypeStruct((t, LANES), F32)]
        blocks += [_nbytes((tm, tn), F32), _nbytes((tm, tn), BF16)]
    return pl.pallas_call(
        functools.partial(_outproj_kernel, final=final),
        grid=grid,
        in_specs=in_specs,
        out_specs=out_specs,
        out_shape=out_shape,
        scratch_shapes=scratch,
        compiler_params=pltpu.CompilerParams(
            dimension_semantics=("parallel", "arbitrary"),
            vmem_limit_bytes=_vmem_limit(*blocks, scratch=2 * _nbytes((tm, tn), F32)
                                         + sum(_nbytes(sc.shape, sc.dtype) for sc in scratch))),
        name="outproj_final" if final else "outproj",
    )(ya, yb, yc, w_bf, x2, g.reshape(1, D_MODEL))


def kernel(x, norm_g, w_in, conv_dw, conv_ln_g, conv_ln_b, conv_pw, attn_sinks, pool_w, pool_scale,
           w_out, final_norm_g):
    b, s, d = x.shape
    assert d == D_MODEL and w_in.shape[-1] == IN_WIDTH and s % 512 == 0
    depth = w_in.shape[0]
    t = b * s
    conv_pw_bf = conv_pw.astype(BF16)
    pool_w_bf = pool_w.astype(BF16)
    w_in_bf = [w_in[0:1].astype(BF16)]
    cast_jobs = [(w_in, l) for l in range(1, depth)] + [(w_out, l) for l in range(depth)]

    x2 = x.reshape(t, d)
    xg, r = _prenorm(x2, norm_g[0])
    for l in range(depth):
        proj, casted = _inproj(xg, w_in_bf[l], 0, r, cast_jobs if l == 0 else ())
        if l == 0:
            w_in_bf += list(casted[:depth - 1])
            w_out_bf = list(casted[depth - 1:])
        proj3 = proj.reshape(b, s, IN_WIDTH)
        ya = _conv_branch(proj3, conv_dw[l], conv_ln_g[l], conv_ln_b[l], conv_pw_bf[l]).reshape(t, CONV_WIDTH)
        yb = _attn_branch(proj3, attn_sinks[l]).reshape(t, ATTN_WIDTH)
        yc = _pool_branch(proj3, pool_w_bf[l], pool_scale[l]).reshape(t, POOL_WIDTH)
        if l + 1 < depth:
            x2, xg, r = _outproj(ya, yb, yc, w_out_bf[l], 0, x2, norm_g[l + 1], final=False, tm=1024, tn=512)
        else:
            x2 = _outproj(ya, yb, yc, w_out_bf[l], 0, x2, final_norm_g, final=True, tm=1024, tn=512)
    return x2.reshape(b, s, d)
```

```python
import functools

import jax
import jax.numpy as jnp
from jax import lax
from jax.experimental import pallas as pl
from jax.experimental.pallas import tpu as pltpu

F32 = jnp.float32
BF16 = jnp.bfloat16

D_MODEL = 4096
CONV_WIDTH = 1024
ATTN_WIDTH = 2048
POOL_WIDTH = 1024
HEAD_DIM = 64
N_Q_HEADS = 32
N_KV_HEADS = 4
Q_PER_KV = N_Q_HEADS // N_KV_HEADS
KV_WIDTH = N_KV_HEADS * HEAD_DIM
WINDOW = 128
BLOCK = 128
CONV_KERNEL = 31
POOL_WINDOWS = (2, 4, 8, 16)
POOL_GROUP = 256
NORM_EPS = 1e-5
LN_EPS = 1e-5
IN_WIDTH = 9728
LOG2E = 1.4426950408889634

LANES = 128
SUBLANES = 8
V7X_VMEM_BYTES = 64 * 1024 * 1024
VMEM_HEADROOM_BYTES = 6 * 1024 * 1024

PROJ_TILE = 512
PROJ_SRC_TILE = (6, 7, 8, 9, 11, 12, 13, 14, 0, 1, 2, 3, 4, 5, 15, 16, 17, 18, 10)
Q_BLK2048, ZB_BLK2048 = 0, 1
A_BLK1024, B_BLK1024, ZA_BLK1024, UC_BLK1024, ZC_BLK1024 = 4, 5, 6, 7, 8
KV_BLK512 = 18

CONV_HALO = 32
POOL_HALO = 16
CONV_STEPS = 64
NORM_ROWS = 32
NORM_UNROLL = 4


def _vmem_limit(*block_bytes, scratch=0):
    need = 2 * sum(block_bytes) + scratch + VMEM_HEADROOM_BYTES
    return min(need, V7X_VMEM_BYTES - 2 * 1024 * 1024)


def _nbytes(shape, dtype):
    n = 1
    for s in shape:
        n *= s
    return n * jnp.dtype(dtype).itemsize


def _silu(z):
    return z * jax.nn.sigmoid(z)


def _prenorm_kernel(x_ref, g_ref, xg_ref, r_ref):
    x = x_ref[...]
    ssq = jnp.sum(x * x, axis=-1, keepdims=True)
    r = lax.rsqrt(ssq * (1.0 / D_MODEL) + NORM_EPS)
    xg_ref[...] = (x * g_ref[...]).astype(BF16)
    r_ref[...] = jnp.broadcast_to(r, r_ref.shape)


def _prenorm(x2, g, *, rows=256):
    t = x2.shape[0]
    return pl.pallas_call(
        _prenorm_kernel,
        grid=(t // rows,),
        in_specs=[pl.BlockSpec((rows, D_MODEL), lambda i: (i, 0)),
                  pl.BlockSpec((1, D_MODEL), lambda i: (0, 0))],
        out_specs=[pl.BlockSpec((rows, D_MODEL), lambda i: (i, 0)),
                   pl.BlockSpec((rows, LANES), lambda i: (i, 0))],
        out_shape=[jax.ShapeDtypeStruct((t, D_MODEL), BF16),
                   jax.ShapeDtypeStruct((t, LANES), F32)],
        compiler_params=pltpu.CompilerParams(
            dimension_semantics=("parallel",),
            vmem_limit_bytes=_vmem_limit(_nbytes((rows, D_MODEL), F32), _nbytes((rows, D_MODEL), BF16))),
        name="prenorm",
    )(x2, g.reshape(1, D_MODEL))


def _inproj_kernel(src_ref, xg_ref, w_ref, r_ref, *rest):
    del src_ref
    n_cast = (len(rest) - 1) // 2
    cast_in, o_ref, cast_out = rest[:n_cast], rest[n_cast], rest[n_cast + 1:]
    acc = jnp.dot(xg_ref[...], w_ref[...], preferred_element_type=F32)
    r = jnp.tile(r_ref[...], (1, acc.shape[1] // LANES))
    o_ref[...] = (acc * r).astype(o_ref.dtype)
    for src_blk, dst_blk in zip(cast_in, cast_out):
        dst_blk[...] = src_blk[...].astype(dst_blk.dtype)


def _inproj(xg, w_bf, layer, r, cast_jobs=(), *, tm=2048):
    t = xg.shape[0]
    n_tiles = IN_WIDTH // PROJ_TILE
    n_rows = t // tm
    src = jnp.asarray(PROJ_SRC_TILE, dtype=jnp.int32)
    in_specs = [pl.BlockSpec((tm, D_MODEL), lambda i, j, s: (i, 0)),
                pl.BlockSpec((None, D_MODEL, PROJ_TILE), lambda i, j, s: (layer, 0, s[j])),
                pl.BlockSpec((tm, LANES), lambda i, j, s: (i, 0))]
    out_specs = [pl.BlockSpec((tm, PROJ_TILE), lambda i, j, s: (i, j))]
    out_shape = [jax.ShapeDtypeStruct((t, IN_WIDTH), BF16)]
    cast_bytes = 0
    for w_f32, w_layer in cast_jobs:
        _, k_dim, cols = w_f32.shape
        blk_rows = k_dim // n_rows
        blk_cols = PROJ_TILE if cols % (PROJ_TILE * n_tiles) == 0 else cols // 16
        last = cols // blk_cols - 1
        in_specs.append(pl.BlockSpec((None, blk_rows, blk_cols),
                                     lambda i, j, s, w_layer=w_layer, last=last: (w_layer, i, jnp.minimum(j, last))))
        out_specs.append(pl.BlockSpec((None, blk_rows, blk_cols),
                                      lambda i, j, s, last=last: (0, i, jnp.minimum(j, last))))
        out_shape.append(jax.ShapeDtypeStruct((1, k_dim, cols), BF16))
        cast_bytes += _nbytes((blk_rows, blk_cols), F32) + _nbytes((blk_rows, blk_cols), BF16)
    grid_spec = pltpu.PrefetchScalarGridSpec(
        num_scalar_prefetch=1, grid=(n_rows, n_tiles), in_specs=in_specs, out_specs=out_specs)
    outs = pl.pallas_call(
        _inproj_kernel,
        grid_spec=grid_spec,
        out_shape=out_shape,
        compiler_params=pltpu.CompilerParams(
            dimension_semantics=("arbitrary", "arbitrary"),
            vmem_limit_bytes=_vmem_limit(_nbytes((tm, D_MODEL), BF16), _nbytes((D_MODEL, PROJ_TILE), BF16),
                                         _nbytes((tm, LANES), F32), _nbytes((tm, PROJ_TILE), BF16), cast_bytes)),
        name="inproj",
    )(src, xg, w_bf, r, *[w for w, _ in cast_jobs])
    return outs[0], outs[1:]


def _slab_pitch(rows):
    p = -(-rows // SUBLANES)
    if p % 2 == 0:
        p += 1
    return p * SUBLANES


def _conv_kernel(a_ref, b_ref, za_ref, ah_ref, bh_ref, dw_ref, lng_ref, lnb_ref, pw_ref, o_ref,
                 u_scr, y_scr, s_scr, *, u_pitch, y_pitch):
    i = pl.program_id(1)
    ts = a_ref.shape[1]
    n_col = CONV_WIDTH // LANES
    for c in range(n_col):
        lanes = slice(c * LANES, (c + 1) * LANES)
        uh = ah_ref[0, :, lanes].astype(F32) * jax.nn.sigmoid(bh_ref[0, :, lanes].astype(F32))
        u_scr[c * u_pitch:c * u_pitch + CONV_HALO, :] = jnp.where(i > 0, uh, 0.0)
        u_scr[c * u_pitch + CONV_HALO:c * u_pitch + CONV_HALO + ts, :] = (
            a_ref[0, :, lanes].astype(F32) * jax.nn.sigmoid(b_ref[0, :, lanes].astype(F32)))

    first_tap = CONV_HALO - (CONV_KERNEL - 1)

    def conv_chunk(ci, carry):
        t0 = ci * CONV_STEPS
        taps_in = [u_scr[pl.ds(t0 + first_tap + k, n_col, stride=u_pitch), :]
                   for k in range(CONV_STEPS + CONV_KERNEL - 1)]
        for r in range(CONV_STEPS):
            acc = dw_ref[0] * taps_in[r]
            for j in range(1, CONV_KERNEL):
                acc = acc + dw_ref[j] * taps_in[r + j]
            y_scr[pl.ds(t0 + r, n_col, stride=y_pitch), :] = acc
        return carry

    lax.fori_loop(0, ts // CONV_STEPS, conv_chunk, 0)

    lng = lng_ref[...]
    lnb = lnb_ref[...]

    def norm_chunk(ci, carry):
        r0 = pl.multiple_of(ci * NORM_ROWS, NORM_ROWS)
        y = jnp.concatenate([y_scr[pl.ds(c * y_pitch + r0, NORM_ROWS), :] for c in range(n_col)], axis=1)
        mu = jnp.mean(y, axis=-1, keepdims=True)
        d = y - mu
        var = jnp.mean(d * d, axis=-1, keepdims=True)
        yn = d * lax.rsqrt(var + LN_EPS) * lng + lnb
        s_scr[pl.ds(r0, NORM_ROWS), :] = _silu(yn).astype(BF16)
        return carry

    lax.fori_loop(0, ts // NORM_ROWS, norm_chunk, 0, unroll=NORM_UNROLL)
    y = jnp.dot(s_scr[...], pw_ref[...], preferred_element_type=F32)
    o_ref[0] = (y * _silu(za_ref[0].astype(F32))).astype(o_ref.dtype)


def _conv_branch(proj3, dw, ln_g, ln_b, pw_bf, *, ts=512):
    b, s, _ = proj3.shape
    halo_per_blk = ts // CONV_HALO
    n_col = CONV_WIDTH // LANES
    u_pitch = _slab_pitch(CONV_HALO + ts)
    y_pitch = _slab_pitch(ts)
    cur = lambda blk: pl.BlockSpec((1, ts, CONV_WIDTH), lambda bi, i: (bi, i, blk))
    halo = lambda blk: pl.BlockSpec((1, CONV_HALO, CONV_WIDTH),
                                    lambda bi, i: (bi, jnp.maximum(i * halo_per_blk - 1, 0), blk))
    full = lambda shape: pl.BlockSpec(shape, lambda bi, i: (0,) * len(shape))
    scratch = [pltpu.VMEM((n_col * u_pitch, LANES), F32),
               pltpu.VMEM((n_col * y_pitch, LANES), F32),
               pltpu.VMEM((ts, CONV_WIDTH), BF16)]
    return pl.pallas_call(
        functools.partial(_conv_kernel, u_pitch=u_pitch, y_pitch=y_pitch),
        grid=(b, s // ts),
        in_specs=[cur(A_BLK1024), cur(B_BLK1024), cur(ZA_BLK1024), halo(A_BLK1024), halo(B_BLK1024),
                  full((CONV_KERNEL, n_col, LANES)), full((1, CONV_WIDTH)), full((1, CONV_WIDTH)),
                  full((CONV_WIDTH, CONV_WIDTH))],
        out_specs=pl.BlockSpec((1, ts, CONV_WIDTH), lambda bi, i: (bi, i, 0)),
        out_shape=jax.ShapeDtypeStruct((b, s, CONV_WIDTH), BF16),
        scratch_shapes=scratch,
        compiler_params=pltpu.CompilerParams(
            dimension_semantics=("parallel", "arbitrary"),
            vmem_limit_bytes=_vmem_limit(4 * _nbytes((ts, CONV_WIDTH), BF16),
                                         2 * _nbytes((CONV_HALO, CONV_WIDTH), BF16),
                                         _nbytes((CONV_WIDTH, CONV_WIDTH), BF16),
                                         _nbytes((CONV_KERNEL + 2, CONV_WIDTH), F32),
                                         scratch=sum(_nbytes(sc.shape, sc.dtype) for sc in scratch)
                                         + 2 * _nbytes((ts, CONV_WIDTH), F32))),
        name="conv_branch",
    )(proj3, proj3, proj3, proj3, proj3, dw.reshape(CONV_KERNEL, n_col, LANES),
      ln_g.reshape(1, -1), ln_b.reshape(1, -1), pw_bf)


def _attn_kernel(sink_ref, q_ref, zb_ref, kvc_ref, kvp_ref, o_ref):
    i = pl.program_id(1)
    tq = q_ref.shape[1]
    row = lax.broadcasted_iota(jnp.int32, (BLOCK, 2 * BLOCK), 0)
    col = lax.broadcasted_iota(jnp.int32, (BLOCK, 2 * BLOCK), 1)
    rel = row + BLOCK - col
    band = (rel >= 0) & (rel <= WINDOW)
    low = lax.broadcasted_iota(jnp.int32, (2 * BLOCK, LANES), 1) < HEAD_DIM
    low_q = lax.broadcasted_iota(jnp.int32, (BLOCK, LANES), 1) < HEAD_DIM
    ones_top = jnp.where(low, 1.0, 0.0).astype(BF16)
    ones_bot = jnp.where(low, 0.0, 1.0).astype(BF16)
    v_right = jnp.concatenate([ones_top, ones_bot], axis=0)

    for qb in range(tq // BLOCK):
        rows = slice(qb * BLOCK, (qb + 1) * BLOCK)
        if qb == 0:
            kv_prev = kvp_ref[0]
            valid = band & (col >= jnp.where(i == 0, BLOCK, 0))
        else:
            kv_prev = kvc_ref[0, (qb - 1) * BLOCK:qb * BLOCK, :]
            valid = band
        bias = jnp.where(valid, 0.0, -jnp.inf).astype(F32)
        bias2 = jnp.concatenate([bias, bias], axis=1)
        kv =jnp.concatenate([kv_prev, kvc_ref[0, rows, :]], axis=0).astype(F32)

        for h in range(N_KV_HEADS):
            def pair_layout(tile):
                swapped = pltpu.roll(tile, HEAD_DIM, axis=1)
                lo_src, hi_src = (tile, swapped) if h % 2 == 0 else (swapped, tile)
                return jnp.concatenate([jnp.where(low, lo_src, 0.0), jnp.where(low, 0.0, hi_src)], axis=0)

            k_tile = kv[:, (h // 2) * LANES:(h // 2 + 1) * LANES]
            v_tile = kv[:, KV_WIDTH + (h // 2) * LANES:KV_WIDTH + (h // 2 + 1) * LANES]
            k_bd = (pair_layout(k_tile) * (HEAD_DIM ** -0.5 * LOG2E)).astype(BF16)
            v_bd = jnp.concatenate([pair_layout(v_tile).astype(BF16), v_right], axis=1)

            for p in range(Q_PER_KV // 2):
                head0 = h * Q_PER_KV + 2 * p
                lanes = slice(head0 * HEAD_DIM, (head0 + 2) * HEAD_DIM)
                s = lax.dot_general(q_ref[0, rows, lanes], k_bd, (((1,), (1,)), ((), ())),
                                    preferred_element_type=F32) + bias2
                m0 = jnp.max(s[:, :2 * BLOCK], axis=-1, keepdims=True)
                m1 = jnp.max(s[:, 2 * BLOCK:], axis=-1, keepdims=True)
                pexp = jnp.concatenate([jnp.exp2(s[:, :2 * BLOCK] - m0), jnp.exp2(s[:, 2 * BLOCK:] - m1)],
                                       axis=1).astype(BF16)
                o_ext = jnp.dot(pexp, v_bd, preferred_element_type=F32)
                sink2 = jnp.where(low_q[0:1], sink_ref[head0] * LOG2E, sink_ref[head0 + 1] * LOG2E)
                sink_term = jnp.exp2(sink2 - jnp.where(low_q, m0, m1))
                attn = o_ext[:, :LANES] / (o_ext[:, LANES:] + sink_term)
                gate = _silu(zb_ref[0, rows, lanes].astype(F32))
                o_ref[0, rows, lanes] = (attn * gate).astype(o_ref.dtype)


def _attn_branch(proj3, sinks, *, tq=256):
    b, s, _ = proj3.shape
    blocks_per_step = tq // BLOCK
    return pl.pallas_call(
        _attn_kernel,
        grid=(b, s // tq),
        in_specs=[pl.BlockSpec(memory_space=pltpu.SMEM),
                  pl.BlockSpec((1, tq, ATTN_WIDTH), lambda bi, i: (bi, i, Q_BLK2048)),
                  pl.BlockSpec((1, tq, ATTN_WIDTH), lambda bi, i: (bi, i, ZB_BLK2048)),
                  pl.BlockSpec((1, tq, 2 * KV_WIDTH), lambda bi, i: (bi, i, KV_BLK512)),
                  pl.BlockSpec((1, BLOCK, 2 * KV_WIDTH),
                               lambda bi, i: (bi, jnp.maximum(i * blocks_per_step - 1, 0), KV_BLK512))],
        out_specs=pl.BlockSpec((1, tq, ATTN_WIDTH), lambda bi, i: (bi, i, 0)),
        out_shape=jax.ShapeDtypeStruct((b, s, ATTN_WIDTH), BF16),
        compiler_params=pltpu.CompilerParams(
            dimension_semantics=("parallel", "arbitrary"),
            vmem_limit_bytes=_vmem_limit(3 * _nbytes((tq, ATTN_WIDTH), BF16),
                                         2 * _nbytes((tq, 2 * KV_WIDTH), BF16),
                                         scratch=8 * 1024 * 1024)),
        name="attn_branch",
    )(sinks, proj3, proj3, proj3, proj3)


def _pool_kernel(u_ref, z_ref, uh_ref, w_ref, sc_ref, o_ref, u_scr):
    i = pl.program_id(1)
    ts = u_ref.shape[1]
    u_scr[0:POOL_HALO, :] = jnp.where(i > 0, uh_ref[0], jnp.zeros_like(uh_ref[0]))
    u_scr[POOL_HALO:POOL_HALO + ts, :] = u_ref[0]
    k_rows = BLOCK + POOL_HALO
    rel = (lax.broadcasted_iota(jnp.int32, (BLOCK, k_rows), 0) + POOL_HALO
           - lax.broadcasted_iota(jnp.int32, (BLOCK, k_rows), 1))
    pos0 = lax.broadcasted_iota(jnp.int32, (BLOCK, POOL_GROUP), 0) + (i * ts + 1)
    for g, win in enumerate(POOL_WINDOWS):
        cols = slice(g * POOL_GROUP, (g + 1) * POOL_GROUP)
        band = jnp.where((rel >= 0) & (rel < win), 1.0, 0.0).astype(BF16)
        mixed = []
        for rb in range(ts // BLOCK):
            tot = jnp.dot(band, u_scr[rb * BLOCK:rb * BLOCK + k_rows, cols], preferred_element_type=F32)
            cur = u_ref[0, rb * BLOCK:(rb + 1) * BLOCK, cols].astype(F32)
            cnt = jnp.minimum((pos0 + rb * BLOCK).astype(F32), float(win))
            mixed.append((tot / cnt - cur).astype(BF16))
        y = jnp.dot(jnp.concatenate(mixed, axis=0), w_ref[g], preferred_element_type=F32)
        gate = _silu(z_ref[0, :, cols].astype(F32))
        o_ref[0, :, cols] = (y * sc_ref[:, cols] * gate).astype(o_ref.dtype)


def _pool_branch(proj3, pool_w_bf, pool_scale, *, ts=512):
    b, s, _ = proj3.shape
    halo_per_blk = ts // POOL_HALO
    return pl.pallas_call(
        _pool_kernel,
        grid=(b, s // ts),
        in_specs=[pl.BlockSpec((1, ts, POOL_WIDTH), lambda bi, i: (bi, i, UC_BLK1024)),
                  pl.BlockSpec((1, ts, POOL_WIDTH), lambda bi, i: (bi, i, ZC_BLK1024)),
                  pl.BlockSpec((1, POOL_HALO, POOL_WIDTH),
                               lambda bi, i: (bi, jnp.maximum(i * halo_per_blk - 1, 0), UC_BLK1024)),
                  pl.BlockSpec((len(POOL_WINDOWS), POOL_GROUP, POOL_GROUP), lambda bi, i: (0, 0, 0)),
                  pl.BlockSpec((1, POOL_WIDTH), lambda bi, i: (0, 0))],
        out_specs=pl.BlockSpec((1, ts, POOL_WIDTH), lambda bi, i: (bi, i, 0)),
        out_shape=jax.ShapeDtypeStruct((b, s, POOL_WIDTH), BF16),
        scratch_shapes=[pltpu.VMEM((POOL_HALO + ts, POOL_WIDTH), BF16)],
        compiler_params=pltpu.CompilerParams(
            dimension_semantics=("parallel", "arbitrary"),
            vmem_limit_bytes=_vmem_limit(3 * _nbytes((ts, POOL_WIDTH), BF16),
                                         scratch=_nbytes((POOL_HALO + ts, POOL_WIDTH), BF16)
                                         + 4 * _nbytes((ts, POOL_WIDTH), F32))),
        name="pool_branch",
    )(proj3, proj3, proj3, pool_w_bf, pool_scale.reshape(1, -1))


def _outproj_kernel(ya_ref, yb_ref, yc_ref, w_ref, x_ref, g_ref, *rest, final):
    i = pl.program_id(0)
    j = pl.program_id(1)
    tn = w_ref.shape[1]
    nj = D_MODEL // tn
    if final:
        o_ref, ssq_scr, xn_scr, r_scr = rest
    else:
        xn_ref, xg_ref, r_ref, ssq_scr = rest

    def row_scale():
        ssq = jnp.sum(ssq_scr[...], axis=-1, keepdims=True)
        return lax.rsqrt(ssq * (1.0 / D_MODEL) + NORM_EPS)

    def matmul_step():
        acc = jnp.dot(ya_ref[...], w_ref[0:CONV_WIDTH, :], preferred_element_type=F32)
        acc = acc + jnp.dot(yb_ref[...], w_ref[CONV_WIDTH:CONV_WIDTH + ATTN_WIDTH, :],
                            preferred_element_type=F32)
        acc = acc + jnp.dot(yc_ref[...], w_ref[CONV_WIDTH + ATTN_WIDTH:, :], preferred_element_type=F32)
        xn = x_ref[...] + acc
        sq = xn * xn
        part = sq[:, 0:LANES]
        for c in range(1, tn // LANES):
            part = part + sq[:, c * LANES:(c + 1) * LANES]

        @pl.when(j == 0)
        def _():
            ssq_scr[...] = part

        @pl.when(j > 0)
        def _():
            ssq_scr[...] = ssq_scr[...] + part

        if final:
            xn_scr[j] = xn

            @pl.when(j == nj - 1)
            def _():
                r_scr[...] = jnp.broadcast_to(row_scale(), r_scr.shape)
        else:
            xn_ref[...] = xn
            xg_ref[...] = (xn * g_ref[...]).astype(BF16)

            @pl.when(j == nj - 1)
            def _():
                r_ref[...] = jnp.broadcast_to(row_scale(), r_ref.shape)

    if final:
        @pl.when(i > 0)
        def _():
            r = jnp.tile(r_scr[...], (1, tn // LANES))
            o_ref[...] = xn_scr[j] * r * g_ref[...]

        pl.when(i < pl.num_programs(0) - 1)(matmul_step)
    else:
        matmul_step()


def _outproj(ya, yb, yc, w_bf, layer, x2, g, *, final, tm, tn):
    t = x2.shape[0]
    nj = D_MODEL // tn
    n_rows = t // tm
    row = (lambda i: jnp.minimum(i, n_rows - 1)) if final else (lambda i: i)
    in_specs = [pl.BlockSpec((tm, CONV_WIDTH), lambda i, j: (row(i), 0)),
                pl.BlockSpec((tm, ATTN_WIDTH), lambda i, j: (row(i), 0)),
                pl.BlockSpec((tm, POOL_WIDTH), lambda i, j: (row(i), 0)),
                pl.BlockSpec((None, D_MODEL, tn), lambda i, j: (layer, 0, j)),
                pl.BlockSpec((tm, tn), lambda i, j: (row(i), j))]
    blocks = [_nbytes((tm, D_MODEL), BF16), _nbytes((D_MODEL, tn), BF16), _nbytes((tm, tn), F32)]
    scratch = [pltpu.VMEM((tm, LANES), F32)]
    if final:
        grid = (n_rows + 1, nj)
        in_specs.append(pl.BlockSpec((1, tn), lambda i, j: (0, j)))
        out_specs = pl.BlockSpec((tm, tn), lambda i, j: (jnp.maximum(i - 1, 0), jnp.where(i == 0, 0, j)))
        out_shape = jax.ShapeDtypeStruct((t, D_MODEL), F32)
        blocks.append(_nbytes((tm, tn), F32))
        scratch += [pltpu.VMEM((nj, tm, tn), F32), pltpu.VMEM((tm, LANES), F32)]
    else:
        grid = (n_rows, nj)
        in_specs.append(pl.BlockSpec((1, tn), lambda i, j: (0, j)))
        out_specs = [pl.BlockSpec((tm, tn), lambda i, j: (i, j)),
                     pl.BlockSpec((tm, tn), lambda i, j: (i, j)),
                     pl.BlockSpec((tm, LANES), lambda i, j: (i, 0))]
        out_shape = [jax.ShapeDtypeStruct((t, D_MODEL), F32),
                     jax.ShapeDtypeStruct((t, D_MODEL), BF16),
                     jax.ShapeDtypeStruct((t, LANES), F32)]
        blocks += [_nbytes((tm, tn), F32), _nbytes((tm, tn), BF16)]
    return pl.pallas_call(
        functools.partial(_outproj_kernel, final=final),
        grid=grid,
        in_specs=in_specs,
        out_specs=out_specs,
        out_shape=out_shape,
        scratch_shapes=scratch,
        compiler_params=pltpu.CompilerParams(
            dimension_semantics=("arbitrary", "arbitrary"),
            vmem_limit_bytes=_vmem_limit(*blocks, scratch=2 * _nbytes((tm, tn), F32)
                                         + sum(_nbytes(sc.shape, sc.dtype) for sc in scratch))),
        name="outproj_final" if final else "outproj",
    )(ya, yb, yc, w_bf, x2, g.reshape(1, D_MODEL))


def kernel(x, norm_g, w_in, conv_dw, conv_ln_g, conv_ln_b, conv_pw, attn_sinks, pool_w, pool_scale,
           w_out, final_norm_g):
    b, s, d = x.shape
    assert d == D_MODEL and w_in.shape[-1] == IN_WIDTH and s % 512 == 0
    depth = w_in.shape[0]
    t = b * s
    conv_pw_bf = conv_pw.astype(BF16)
    pool_w_bf = pool_w.astype(BF16)
    w_in_bf = [w_in[0:1].astype(BF16)]
    cast_jobs = [(w_in, l) for l in range(1, depth)] + [(w_out, l) for l in range(depth)]

    x2 = x.reshape(t, d)
    xg, r = _prenorm(x2, norm_g[0])
    for l in range(depth):
        proj, casted = _inproj(xg, w_in_bf[l], 0, r, cast_jobs if l == 0 else ())
        if l == 0:
            w_in_bf += list(casted[:depth - 1])
            w_out_bf = list(casted[depth - 1:])
        proj3 = proj.reshape(b, s, IN_WIDTH)
        ya = _conv_branch(proj3, conv_dw[l], conv_ln_g[l], conv_ln_b[l], conv_pw_bf[l]).reshape(t, CONV_WIDTH)
        yb = _attn_branch(proj3, attn_sinks[l]).reshape(t, ATTN_WIDTH)
        yc = _pool_branch(proj3, pool_w_bf[l], pool_scale[l]).reshape(t, POOL_WIDTH)
        if l + 1 < depth:
            x2, xg, r = _outproj(ya, yb, yc, w_out_bf[l], 0, x2, norm_g[l + 1], final=False, tm=1024, tn=512)
        else:
            x2 = _outproj(ya, yb, yc, w_out_bf[l], 0, x2, final_norm_g, final=True, tm=1024, tn=512)
    return x2.reshape(b, s, d)
```

```python
import functools

import jax
import jax.numpy as jnp
from jax import lax
from jax.experimental import pallas as pl
from jax.experimental.pallas import tpu as pltpu

F32 = jnp.float32
BF16 = jnp.bfloat16

D_MODEL = 4096
CONV_WIDTH = 1024
ATTN_WIDTH = 2048
POOL_WIDTH = 1024
HEAD_DIM = 64
N_Q_HEADS = 32
N_KV_HEADS = 4
Q_PER_KV = N_Q_HEADS // N_KV_HEADS
KV_WIDTH = N_KV_HEADS * HEAD_DIM
WINDOW = 128
BLOCK = 128
CONV_KERNEL = 31
POOL_WINDOWS = (2, 4, 8, 16)
POOL_GROUP = 256
NORM_EPS = 1e-5
LN_EPS = 1e-5
IN_WIDTH = 9728
LOG2E = 1.4426950408889634

LANES = 128
SUBLANES = 8
V7X_VMEM_BYTES = 64 * 1024 * 1024
VMEM_HEADROOM_BYTES = 6 * 1024 * 1024

PROJ_TILE = 512
PROJ_SRC_TILE = (6, 7, 8, 9, 11, 12, 13, 14, 0, 1, 2, 3, 4, 5, 15, 16, 17, 18, 10)
Q_BLK2048, ZB_BLK2048 = 0, 1
A_BLK1024, B_BLK1024, ZA_BLK1024, UC_BLK1024, ZC_BLK1024 = 4, 5, 6, 7, 8
KV_BLK512 = 18

CONV_HALO = 32
POOL_HALO = 16
CONV_STEPS = 64
NORM_ROWS = 32
NORM_UNROLL = 4


def _vmem_limit(*block_bytes, scratch=0):
    need = 2 * sum(block_bytes) + scratch + VMEM_HEADROOM_BYTES
    return min(need, V7X_VMEM_BYTES - 2 * 1024 * 1024)


def _nbytes(shape, dtype):
    n = 1
    for s in shape:
        n *= s
    return n * jnp.dtype(dtype).itemsize


def _silu(z):
    return z * jax.nn.sigmoid(z)


def _prenorm_kernel(x_ref, g_ref, xg_ref, r_ref):
    x = x_ref[...]
    ssq = jnp.sum(x * x, axis=-1, keepdims=True)
    r = lax.rsqrt(ssq * (1.0 / D_MODEL) + NORM_EPS)
    xg_ref[...] = (x * g_ref[...]).astype(BF16)
    r_ref[...] = jnp.broadcast_to(r, r_ref.shape)


def _prenorm(x2, g, *, rows=256):
    t = x2.shape[0]
    return pl.pallas_call(
        _prenorm_kernel,
        grid=(t // rows,),
        in_specs=[pl.BlockSpec((rows, D_MODEL), lambda i: (i, 0)),
                  pl.BlockSpec((1, D_MODEL), lambda i: (0, 0))],
        out_specs=[pl.BlockSpec((rows, D_MODEL), lambda i: (i, 0)),
                   pl.BlockSpec((rows, LANES), lambda i: (i, 0))],
        out_shape=[jax.ShapeDtypeStruct((t, D_MODEL), BF16),
                   jax.ShapeDtypeStruct((t, LANES), F32)],
        compiler_params=pltpu.CompilerParams(
            dimension_semantics=("parallel",),
            vmem_limit_bytes=_vmem_limit(_nbytes((rows, D_MODEL), F32), _nbytes((rows, D_MODEL), BF16))),
        name="prenorm",
    )(x2, g.reshape(1, D_MODEL))


def _inproj_kernel(src_ref, xg_ref, w_ref, r_ref, *rest):
    del src_ref
    n_cast = (len(rest) - 1) // 2
    cast_in, o_ref, cast_out = rest[:n_cast], rest[n_cast], rest[n_cast + 1:]
    acc = jnp.dot(xg_ref[...], w_ref[...], preferred_element_type=F32)
    r = jnp.tile(r_ref[...], (1, acc.shape[1] // LANES))
    o_ref[...] = (acc * r).astype(o_ref.dtype)
    for src_blk, dst_blk in zip(cast_in, cast_out):
        dst_blk[...] = src_blk[...].astype(dst_blk.dtype)


def _inproj(xg, w_bf, layer, r, cast_jobs=(), *, tm=2048):
    t = xg.shape[0]
    n_tiles = IN_WIDTH // PROJ_TILE
    n_rows = t // tm
    src = jnp.asarray(PROJ_SRC_TILE, dtype=jnp.int32)
    in_specs = [pl.BlockSpec((tm, D_MODEL), lambda i, j, s: (i, 0)),
                pl.BlockSpec((None, D_MODEL, PROJ_TILE), lambda i, j, s: (layer, 0, s[j])),
                pl.BlockSpec((tm, LANES), lambda i, j, s: (i, 0))]
    out_specs = [pl.BlockSpec((tm, PROJ_TILE), lambda i, j, s: (i, j))]
    out_shape = [jax.ShapeDtypeStruct((t, IN_WIDTH), BF16)]
    cast_bytes = 0
    for w_f32, w_layer in cast_jobs:
        _, k_dim, cols = w_f32.shape
        blk_rows = k_dim // n_rows
        blk_cols = PROJ_TILE if cols % (PROJ_TILE * n_tiles) == 0 else cols // 16
        last = cols // blk_cols - 1
        in_specs.append(pl.BlockSpec((None, blk_rows, blk_cols),
                                     lambda i, j, s, w_layer=w_layer, last=last: (w_layer, i, jnp.minimum(j, last))))
        out_specs.append(pl.BlockSpec((None, blk_rows, blk_cols),
                                      lambda i, j, s, last=last: (0, i, jnp.minimum(j, last))))
        out_shape.append(jax.ShapeDtypeStruct((1, k_dim, cols), BF16))
        cast_bytes += _nbytes((blk_rows, blk_cols), F32) + _nbytes((blk_rows, blk_cols), BF16)
    grid_spec = pltpu.PrefetchScalarGridSpec(
        num_scalar_prefetch=1, grid=(n_rows, n_tiles), in_specs=in_specs, out_specs=out_specs)
    outs = pl.pallas_call(
        _inproj_kernel,
        grid_spec=grid_spec,
        out_shape=out_shape,
        compiler_params=pltpu.CompilerParams(
            dimension_semantics=("arbitrary", "arbitrary"),
            vmem_limit_bytes=_vmem_limit(_nbytes((tm, D_MODEL), BF16), _nbytes((D_MODEL, PROJ_TILE), BF16),
                                         _nbytes((tm, LANES), F32), _nbytes((tm, PROJ_TILE), BF16), cast_bytes)),
        name="inproj",
    )(src, xg, w_bf, r, *[w for w, _ in cast_jobs])
    return outs[0], outs[1:]


def _slab_pitch(rows):
    p = -(-rows // SUBLANES)
    if p % 2 == 0:
        p += 1
    return p * SUBLANES


def _conv_kernel(a_ref, b_ref, za_ref, ah_ref, bh_ref, dw_ref, lng_ref, lnb_ref, pw_ref, o_ref,
                 u_scr, y_scr, s_scr, *, u_pitch, y_pitch):
    i = pl.program_id(1)
    ts = a_ref.shape[1]
    n_col = CONV_WIDTH // LANES
    for c in range(n_col):
        lanes = slice(c * LANES, (c + 1) * LANES)
        uh = ah_ref[0, :, lanes].astype(F32) * jax.nn.sigmoid(bh_ref[0, :, lanes].astype(F32))
        u_scr[c * u_pitch:c * u_pitch + CONV_HALO, :] = jnp.where(i > 0, uh, 0.0)
        u_scr[c * u_pitch + CONV_HALO:c * u_pitch + CONV_HALO + ts, :] = (
            a_ref[0, :, lanes].astype(F32) * jax.nn.sigmoid(b_ref[0, :, lanes].astype(F32)))

    first_tap = CONV_HALO - (CONV_KERNEL - 1)

    def conv_chunk(ci, carry):
        t0 = ci * CONV_STEPS
        taps_in = [u_scr[pl.ds(t0 + first_tap + k, n_col, stride=u_pitch), :]
                   for k in range(CONV_STEPS + CONV_KERNEL - 1)]
        for r in range(CONV_STEPS):
            acc = dw_ref[0] * taps_in[r]
            for j in range(1, CONV_KERNEL):
                acc = acc + dw_ref[j] * taps_in[r + j]
            y_scr[pl.ds(t0 + r, n_col, stride=y_pitch), :] = acc
        return carry

    lax.fori_loop(0, ts // CONV_STEPS, conv_chunk, 0)

    lng = lng_ref[...]
    lnb = lnb_ref[...]

    for ci in range(ts // BLOCK):
        parts = []
        for sub in range(BLOCK // NORM_ROWS):
            r0 = ci * BLOCK + sub * NORM_ROWS
            y = jnp.concatenate([y_scr[c * y_pitch + r0:c * y_pitch + r0 + NORM_ROWS, :] for c in range(n_col)],
                                axis=1)
            mu = jnp.mean(y, axis=-1, keepdims=True)
            d = y - mu
            var = jnp.mean(d * d, axis=-1, keepdims=True)
            yn = d * lax.rsqrt(var + LN_EPS) * lng + lnb
            parts.append(_silu(yn).astype(BF16))
        rows = slice(ci * BLOCK, (ci + 1) * BLOCK)
        y = jnp.dot(jnp.concatenate(parts, axis=0), pw_ref[...], preferred_element_type=F32)
        o_ref[0, rows, :] = (y * _silu(za_ref[0, rows, :].astype(F32))).astype(o_ref.dtype)


def _conv_branch(proj3, dw, ln_g, ln_b, pw_bf, *, ts=512):
    b, s, _ = proj3.shape
    halo_per_blk = ts // CONV_HALO
    n_col = CONV_WIDTH // LANES
    u_pitch = _slab_pitch(CONV_HALO + ts)
    y_pitch = _slab_pitch(ts)
    cur = lambda blk: pl.BlockSpec((1, ts, CONV_WIDTH), lambda bi, i: (bi, i, blk))
    halo = lambda blk: pl.BlockSpec((1, CONV_HALO, CONV_WIDTH),
                                    lambda bi, i: (bi, jnp.maximum(i * halo_per_blk - 1, 0), blk))
    full = lambda shape: pl.BlockSpec(shape, lambda bi, i: (0,) * len(shape))
    scratch = [pltpu.VMEM((n_col * u_pitch, LANES), F32),
               pltpu.VMEM((n_col * y_pitch, LANES), F32),
               pltpu.VMEM((ts, CONV_WIDTH), BF16)]
    return pl.pallas_call(
        functools.partial(_conv_kernel, u_pitch=u_pitch, y_pitch=y_pitch),
        grid=(b, s // ts),
        in_specs=[cur(A_BLK1024), cur(B_BLK1024), cur(ZA_BLK1024), halo(A_BLK1024), halo(B_BLK1024),
                  full((CONV_KERNEL, n_col, LANES)), full((1, CONV_WIDTH)), full((1, CONV_WIDTH)),
                  full((CONV_WIDTH, CONV_WIDTH))],
        out_specs=pl.BlockSpec((1, ts, CONV_WIDTH), lambda bi, i: (bi, i, 0)),
        out_shape=jax.ShapeDtypeStruct((b, s, CONV_WIDTH), BF16),
        scratch_shapes=scratch,
        compiler_params=pltpu.CompilerParams(
            dimension_semantics=("parallel", "arbitrary"),
            vmem_limit_bytes=_vmem_limit(4 * _nbytes((ts, CONV_WIDTH), BF16),
                                         2 * _nbytes((CONV_HALO, CONV_WIDTH), BF16),
                                         _nbytes((CONV_WIDTH, CONV_WIDTH), BF16),
                                         _nbytes((CONV_KERNEL + 2, CONV_WIDTH), F32),
                                         scratch=sum(_nbytes(sc.shape, sc.dtype) for sc in scratch)
                                         + 2 * _nbytes((ts, CONV_WIDTH), F32))),
        name="conv_branch",
    )(proj3, proj3, proj3, proj3, proj3, dw.reshape(CONV_KERNEL, n_col, LANES),
      ln_g.reshape(1, -1), ln_b.reshape(1, -1), pw_bf)


def _attn_kernel(sink_ref, q_ref, zb_ref, kvc_ref, kvp_ref, o_ref):
    i = pl.program_id(1)
    tq = q_ref.shape[1]
    row = lax.broadcasted_iota(jnp.int32, (BLOCK, 2 * BLOCK), 0)
    col = lax.broadcasted_iota(jnp.int32, (BLOCK, 2 * BLOCK), 1)
    rel = row + BLOCK - col
    band = (rel >= 0) & (rel <= WINDOW)
    low = lax.broadcasted_iota(jnp.int32, (2 * BLOCK, LANES), 1) < HEAD_DIM
    low_q = lax.broadcasted_iota(jnp.int32, (BLOCK, LANES), 1) < HEAD_DIM
    ones_top = jnp.where(low, 1.0, 0.0).astype(BF16)
    ones_bot = jnp.where(low, 0.0, 1.0).astype(BF16)
    v_right = jnp.concatenate([ones_top, ones_bot], axis=0)

    for qb in range(tq // BLOCK):
        rows = slice(qb * BLOCK, (qb + 1) * BLOCK)
        if qb == 0:
            kv_prev = kvp_ref[0]
            valid = band & (col >= jnp.where(i == 0, BLOCK, 0))
        else:
            kv_prev = kvc_ref[0, (qb - 1) * BLOCK:qb * BLOCK, :]
            valid = band
        bias = jnp.where(valid, 0.0, -jnp.inf).astype(F32)
        bias2 = jnp.concatenate([bias, bias], axis=1)
        kv =jnp.concatenate([kv_prev, kvc_ref[0, rows, :]], axis=0).astype(F32)

        for h in range(N_KV_HEADS):
            def pair_layout(tile):
                swapped = pltpu.roll(tile, HEAD_DIM, axis=1)
                lo_src, hi_src = (tile, swapped) if h % 2 == 0 else (swapped, tile)
                return jnp.concatenate([jnp.where(low, lo_src, 0.0), jnp.where(low, 0.0, hi_src)], axis=0)

            k_tile = kv[:, (h // 2) * LANES:(h // 2 + 1) * LANES]
            v_tile = kv[:, KV_WIDTH + (h // 2) * LANES:KV_WIDTH + (h // 2 + 1) * LANES]
            k_bd = (pair_layout(k_tile) * (HEAD_DIM ** -0.5 * LOG2E)).astype(BF16)
            v_bd = jnp.concatenate([pair_layout(v_tile).astype(BF16), v_right], axis=1)

            for p in range(Q_PER_KV // 2):
                head0 = h * Q_PER_KV + 2 * p
                lanes = slice(head0 * HEAD_DIM, (head0 + 2) * HEAD_DIM)
                s = lax.dot_general(q_ref[0, rows, lanes], k_bd, (((1,), (1,)), ((), ())),
                                    preferred_element_type=F32) + bias2
                m0 = jnp.max(s[:, :2 * BLOCK], axis=-1, keepdims=True)
                m1 = jnp.max(s[:, 2 * BLOCK:], axis=-1, keepdims=True)
                pexp = jnp.concatenate([jnp.exp2(s[:, :2 * BLOCK] - m0), jnp.exp2(s[:, 2 * BLOCK:] - m1)],
                                       axis=1).astype(BF16)
                o_ext = jnp.dot(pexp, v_bd, preferred_element_type=F32)
                sink2 = jnp.where(low_q[0:1], sink_ref[head0] * LOG2E, sink_ref[head0 + 1] * LOG2E)
                sink_term = jnp.exp2(sink2 - jnp.where(low_q, m0, m1))
                attn = o_ext[:, :LANES] / (o_ext[:, LANES:] + sink_term)
                gate = _silu(zb_ref[0, rows, lanes].astype(F32))
                o_ref[0, rows, lanes] = (attn * gate).astype(o_ref.dtype)


def _attn_branch(proj3, sinks, *, tq=512):
    b, s, _ = proj3.shape
    blocks_per_step = tq // BLOCK
    return pl.pallas_call(
        _attn_kernel,
        grid=(b, s // tq),
        in_specs=[pl.BlockSpec(memory_space=pltpu.SMEM),
                  pl.BlockSpec((1, tq, ATTN_WIDTH), lambda bi, i: (bi, i, Q_BLK2048)),
                  pl.BlockSpec((1, tq, ATTN_WIDTH), lambda bi, i: (bi, i, ZB_BLK2048)),
                  pl.BlockSpec((1, tq, 2 * KV_WIDTH), lambda bi, i: (bi, i, KV_BLK512)),
                  pl.BlockSpec((1, BLOCK, 2 * KV_WIDTH),
                               lambda bi, i: (bi, jnp.maximum(i * blocks_per_step - 1, 0), KV_BLK512))],
        out_specs=pl.BlockSpec((1, tq, ATTN_WIDTH), lambda bi, i: (bi, i, 0)),
        out_shape=jax.ShapeDtypeStruct((b, s, ATTN_WIDTH), BF16),
        compiler_params=pltpu.CompilerParams(
            dimension_semantics=("parallel", "arbitrary"),
            vmem_limit_bytes=_vmem_limit(3 * _nbytes((tq, ATTN_WIDTH), BF16),
                                         2 * _nbytes((tq, 2 * KV_WIDTH), BF16),
                                         scratch=8 * 1024 * 1024)),
        name="attn_branch",
    )(sinks, proj3, proj3, proj3, proj3)


def _pool_kernel(u_ref, z_ref, uh_ref, w_ref, sc_ref, o_ref, u_scr):
    i = pl.program_id(1)
    ts = u_ref.shape[1]
    u_scr[0:POOL_HALO, :] = jnp.where(i > 0, uh_ref[0], jnp.zeros_like(uh_ref[0]))
    u_scr[POOL_HALO:POOL_HALO + ts, :] = u_ref[0]
    k_rows = BLOCK + POOL_HALO
    rel = (lax.broadcasted_iota(jnp.int32, (BLOCK, k_rows), 0) + POOL_HALO
           - lax.broadcasted_iota(jnp.int32, (BLOCK, k_rows), 1))
    pos0 = lax.broadcasted_iota(jnp.int32, (BLOCK, POOL_GROUP), 0) + (i * ts + 1)
    for g, win in enumerate(POOL_WINDOWS):
        cols = slice(g * POOL_GROUP, (g + 1) * POOL_GROUP)
        band = jnp.where((rel >= 0) & (rel < win), 1.0, 0.0).astype(BF16)
        mixed = []
        for rb in range(ts // BLOCK):
            tot = jnp.dot(band, u_scr[rb * BLOCK:rb * BLOCK + k_rows, cols], preferred_element_type=F32)
            cur = u_ref[0, rb * BLOCK:(rb + 1) * BLOCK, cols].astype(F32)
            cnt = jnp.minimum((pos0 + rb * BLOCK).astype(F32), float(win))
            mixed.append((tot / cnt - cur).astype(BF16))
        y = jnp.dot(jnp.concatenate(mixed, axis=0), w_ref[g], preferred_element_type=F32)
        gate = _silu(z_ref[0, :, cols].astype(F32))
        o_ref[0, :, cols] = (y * sc_ref[:, cols] * gate).astype(o_ref.dtype)


def _pool_branch(proj3, pool_w_bf, pool_scale, *, ts=512):
    b, s, _ = proj3.shape
    halo_per_blk = ts // POOL_HALO
    return pl.pallas_call(
        _pool_kernel,
        grid=(b, s // ts),
        in_specs=[pl.BlockSpec((1, ts, POOL_WIDTH), lambda bi, i: (bi, i, UC_BLK1024)),
                  pl.BlockSpec((1, ts, POOL_WIDTH), lambda bi, i: (bi, i, ZC_BLK1024)),
                  pl.BlockSpec((1, POOL_HALO, POOL_WIDTH),
                               lambda bi, i: (bi, jnp.maximum(i * halo_per_blk - 1, 0), UC_BLK1024)),
                  pl.BlockSpec((len(POOL_WINDOWS), POOL_GROUP, POOL_GROUP), lambda bi, i: (0, 0, 0)),
                  pl.BlockSpec((1, POOL_WIDTH), lambda bi, i: (0, 0))],
        out_specs=pl.BlockSpec((1, ts, POOL_WIDTH), lambda bi, i: (bi, i, 0)),
        out_shape=jax.ShapeDtypeStruct((b, s, POOL_WIDTH), BF16),
        scratch_shapes=[pltpu.VMEM((POOL_HALO + ts, POOL_WIDTH), BF16)],
        compiler_params=pltpu.CompilerParams(
            dimension_semantics=("parallel", "arbitrary"),
            vmem_limit_bytes=_vmem_limit(3 * _nbytes((ts, POOL_WIDTH), BF16),
                                         scratch=_nbytes((POOL_HALO + ts, POOL_WIDTH), BF16)
                                         + 4 * _nbytes((ts, POOL_WIDTH), F32))),
        name="pool_branch",
    )(proj3, proj3, proj3, pool_w_bf, pool_scale.reshape(1, -1))


def _outproj_kernel(ya_ref, yb_ref, yc_ref, w_ref, x_ref, g_ref, *rest, final):
    i = pl.program_id(0)
    j = pl.program_id(1)
    tn = w_ref.shape[1]
    nj = D_MODEL // tn
    if final:
        o_ref, ssq_scr, xn_scr, r_scr = rest
    else:
        xn_ref, xg_ref, r_ref, ssq_scr = rest

    def row_scale():
        ssq = jnp.sum(ssq_scr[...], axis=-1, keepdims=True)
        return lax.rsqrt(ssq * (1.0 / D_MODEL) + NORM_EPS)

    def matmul_step():
        acc = jnp.dot(ya_ref[...], w_ref[0:CONV_WIDTH, :], preferred_element_type=F32)
        acc = acc + jnp.dot(yb_ref[...], w_ref[CONV_WIDTH:CONV_WIDTH + ATTN_WIDTH, :],
                            preferred_element_type=F32)
        acc = acc + jnp.dot(yc_ref[...], w_ref[CONV_WIDTH + ATTN_WIDTH:, :], preferred_element_type=F32)
        xn = x_ref[...] + acc
        sq = xn * xn
        part = sq[:, 0:LANES]
        for c in range(1, tn // LANES):
            part = part + sq[:, c * LANES:(c + 1) * LANES]

        @pl.when(j == 0)
        def _():
            ssq_scr[...] = part

        @pl.when(j > 0)
        def _():
            ssq_scr[...] = ssq_scr[...] + part

        if final:
            xn_scr[j] = xn

            @pl.when(j == nj - 1)
            def _():
                r_scr[...] = jnp.broadcast_to(row_scale(), r_scr.shape)
        else:
            xn_ref[...] = xn
            xg_ref[...] = (xn * g_ref[...]).astype(BF16)

            @pl.when(j == nj - 1)
            def _():
                r_ref[...] = jnp.broadcast_to(row_scale(), r_ref.shape)

    if final:
        @pl.when(i > 0)
        def _():
            r = jnp.tile(r_scr[...], (1, tn // LANES))
            o_ref[...] = xn_scr[j] * r * g_ref[...]

        pl.when(i < pl.num_programs(0) - 1)(matmul_step)
    else:
        matmul_step()


def _outproj(ya, yb, yc, w_bf, layer, x2, g, *, final, tm, tn):
    t = x2.shape[0]
    nj = D_MODEL // tn
    n_rows = t // tm
    row = (lambda i: jnp.minimum(i, n_rows - 1)) if final else (lambda i: i)
    in_specs = [pl.BlockSpec((tm, CONV_WIDTH), lambda i, j: (row(i), 0)),
                pl.BlockSpec((tm, ATTN_WIDTH), lambda i, j: (row(i), 0)),
                pl.BlockSpec((tm, POOL_WIDTH), lambda i, j: (row(i), 0)),
                pl.BlockSpec((None, D_MODEL, tn), lambda i, j: (layer, 0, j)),
                pl.BlockSpec((tm, tn), lambda i, j: (row(i), j))]
    blocks = [_nbytes((tm, D_MODEL), BF16), _nbytes((D_MODEL, tn), BF16), _nbytes((tm, tn), F32)]
    scratch = [pltpu.VMEM((tm, LANES), F32)]
    if final:
        grid = (n_rows + 1, nj)
        in_specs.append(pl.BlockSpec((1, tn), lambda i, j: (0, j)))
        out_specs = pl.BlockSpec((tm, tn), lambda i, j: (jnp.maximum(i - 1, 0), jnp.where(i == 0, 0, j)))
        out_shape = jax.ShapeDtypeStruct((t, D_MODEL), F32)
        blocks.append(_nbytes((tm, tn), F32))
        scratch += [pltpu.VMEM((nj, tm, tn), F32), pltpu.VMEM((tm, LANES), F32)]
    else:
        grid = (n_rows, nj)
        in_specs.append(pl.BlockSpec((1, tn), lambda i, j: (0, j)))
        out_specs = [pl.BlockSpec((tm, tn), lambda i, j: (i, j)),
                     pl.BlockSpec((tm, tn), lambda i, j: (i, j)),
                     pl.BlockSpec((tm, LANES), lambda i, j: (i, 0))]
        out_shape = [jax.ShapeDtypeStruct((t, D_MODEL), F32),
                     jax.ShapeDtypeStruct((t, D_MODEL), BF16),
                     jax.ShapeDtypeStruct((t, LANES), F32)]
        blocks += [_nbytes((tm, tn), F32), _nbytes((tm, tn), BF16)]
    return pl.pallas_call(
        functools.partial(_outproj_kernel, final=final),
        grid=grid,
        in_specs=in_specs,
        out_specs=out_specs,
        out_shape=out_shape,
        scratch_shapes=scratch,
        compiler_params=pltpu.CompilerParams(
            dimension_semantics=("arbitrary", "arbitrary"),
            vmem_limit_bytes=_vmem_limit(*blocks, scratch=2 * _nbytes((tm, tn), F32)
                                         + sum(_nbytes(sc.shape, sc.dtype) for sc in scratch))),
        name="outproj_final" if final else "outproj",
    )(ya, yb, yc, w_bf, x2, g.reshape(1, D_MODEL))


def kernel(x, norm_g, w_in, conv_dw, conv_ln_g, conv_ln_b, conv_pw, attn_sinks, pool_w, pool_scale,
           w_out, final_norm_g):
    b, s, d = x.shape
    assert d == D_MODEL and w_in.shape[-1] == IN_WIDTH and s % 512 == 0
    depth = w_in.shape[0]
    t = b * s
    conv_pw_bf = conv_pw.astype(BF16)
    pool_w_bf = pool_w.astype(BF16)
    w_in_bf = [w_in[0:1].astype(BF16)]
    cast_jobs = [(w_in, l) for l in range(1, depth)] + [(w_out, l) for l in range(depth)]

    x2 = x.reshape(t, d)
    xg, r = _prenorm(x2, norm_g[0])
    for l in range(depth):
        proj, casted = _inproj(xg, w_in_bf[l], 0, r, cast_jobs if l == 0 else ())
        if l == 0:
            w_in_bf += list(casted[:depth - 1])
            w_out_bf = list(casted[depth - 1:])
        proj3 = proj.reshape(b, s, IN_WIDTH)
        ya = _conv_branch(proj3, conv_dw[l], conv_ln_g[l], conv_ln_b[l], conv_pw_bf[l]).reshape(t, CONV_WIDTH)
        yb = _attn_branch(proj3, attn_sinks[l]).reshape(t, ATTN_WIDTH)
        yc = _pool_branch(proj3, pool_w_bf[l], pool_scale[l]).reshape(t, POOL_WIDTH)
        if l + 1 < depth:
            x2, xg, r = _outproj(ya, yb, yc, w_out_bf[l], 0, x2, norm_g[l + 1], final=False, tm=1024, tn=512)
        else:
            x2 = _outproj(ya, yb, yc, w_out_bf[l], 0, x2, final_norm_g, final=True, tm=1024, tn=512)
    return x2.reshape(b, s, d)
```

```python
import functools

import jax
import jax.numpy as jnp
from jax import lax
from jax.experimental import pallas as pl
from jax.experimental.pallas import tpu as pltpu

F32 = jnp.float32
BF16 = jnp.bfloat16

D_MODEL = 4096
CONV_WIDTH = 1024
ATTN_WIDTH = 2048
POOL_WIDTH = 1024
HEAD_DIM = 64
N_Q_HEADS = 32
N_KV_HEADS = 4
Q_PER_KV = N_Q_HEADS // N_KV_HEADS
KV_WIDTH = N_KV_HEADS * HEAD_DIM
WINDOW = 128
BLOCK = 128
CONV_KERNEL = 31
POOL_WINDOWS = (2, 4, 8, 16)
POOL_GROUP = 256
NORM_EPS = 1e-5
LN_EPS = 1e-5
IN_WIDTH = 9728
LOG2E = 1.4426950408889634

LANES = 128
SUBLANES = 8
V7X_VMEM_BYTES = 64 * 1024 * 1024
VMEM_HEADROOM_BYTES = 6 * 1024 * 1024
VMEM_RESERVE_BYTES = 2 * 1024 * 1024

PROJ_TILE = 512
PROJ_SRC_TILE = (6, 7, 8, 9, 11, 12, 13, 14, 0, 1, 2, 3, 4, 5, 15, 16, 17, 18, 10)
Q_BLK2048, ZB_BLK2048 = 0, 1
A_BLK1024, B_BLK1024, ZA_BLK1024, UC_BLK1024, ZC_BLK1024 = 4, 5, 6, 7, 8
KV_BLK512 = 18

CONV_HALO = 32
POOL_HALO = 16
CONV_STEPS = 64
NORM_ROWS = 32


def _vmem_limit(*block_bytes, scratch=0):
    need = 2 * sum(block_bytes) + scratch + VMEM_HEADROOM_BYTES
    return min(need, V7X_VMEM_BYTES - VMEM_RESERVE_BYTES)


def _nbytes(shape, dtype):
    n = 1
    for s in shape:
        n *= s
    return n * jnp.dtype(dtype).itemsize


def _silu(z):
    return z * jax.nn.sigmoid(z)


def _prenorm_kernel(x_ref, g_ref, xg_ref, r_ref):
    x = x_ref[...]
    ssq = jnp.sum(x * x, axis=-1, keepdims=True)
    r = lax.rsqrt(ssq * (1.0 / D_MODEL) + NORM_EPS)
    xg_ref[...] = (x * g_ref[...]).astype(BF16)
    r_ref[...] = jnp.broadcast_to(r, r_ref.shape)


def _prenorm(x2, g, *, rows=256):
    t = x2.shape[0]
    return pl.pallas_call(
        _prenorm_kernel,
        grid=(t // rows,),
        in_specs=[pl.BlockSpec((rows, D_MODEL), lambda i: (i, 0)),
                  pl.BlockSpec((1, D_MODEL), lambda i: (0, 0))],
        out_specs=[pl.BlockSpec((rows, D_MODEL), lambda i: (i, 0)),
                   pl.BlockSpec((rows, LANES), lambda i: (i, 0))],
        out_shape=[jax.ShapeDtypeStruct((t, D_MODEL), BF16),
                   jax.ShapeDtypeStruct((t, LANES), F32)],
        compiler_params=pltpu.CompilerParams(
            dimension_semantics=("parallel",),
            vmem_limit_bytes=_vmem_limit(_nbytes((rows, D_MODEL), F32), _nbytes((rows, D_MODEL), BF16))),
        name="prenorm",
    )(x2, g.reshape(1, D_MODEL))


def _inproj_kernel(src_ref, xg_ref, w_ref, r_ref, *rest):
    del src_ref
    n_cast = (len(rest) - 1) // 2
    cast_in, o_ref, cast_out = rest[:n_cast], rest[n_cast], rest[n_cast + 1:]
    acc = jnp.dot(xg_ref[...], w_ref[...], preferred_element_type=F32)
    r = jnp.tile(r_ref[...], (1, acc.shape[1] // LANES))
    o_ref[...] = (acc * r).astype(o_ref.dtype)
    for src_blk, dst_blk in zip(cast_in, cast_out):
        dst_blk[...] = src_blk[...].astype(dst_blk.dtype)


def _inproj(xg, w_bf, layer, r, cast_jobs=(), *, tm=2048):
    t = xg.shape[0]
    n_tiles = IN_WIDTH // PROJ_TILE
    n_rows = t // tm
    src = jnp.asarray(PROJ_SRC_TILE, dtype=jnp.int32)
    in_specs = [pl.BlockSpec((tm, D_MODEL), lambda i, j, s: (i, 0)),
                pl.BlockSpec((None, D_MODEL, PROJ_TILE), lambda i, j, s: (layer, 0, s[j])),
                pl.BlockSpec((tm, LANES), lambda i, j, s: (i, 0))]
    out_specs = [pl.BlockSpec((tm, PROJ_TILE), lambda i, j, s: (i, j))]
    out_shape = [jax.ShapeDtypeStruct((t, IN_WIDTH), BF16)]
    cast_bytes = 0
    for w_f32, w_layer in cast_jobs:
        _, k_dim, cols = w_f32.shape
        blk_rows = k_dim // n_rows
        blk_cols = PROJ_TILE if cols % (PROJ_TILE * n_tiles) == 0 else cols // 16
        last = cols // blk_cols - 1
        in_specs.append(pl.BlockSpec((None, blk_rows, blk_cols),
                                     lambda i, j, s, w_layer=w_layer, last=last: (w_layer, i, jnp.minimum(j, last))))
        out_specs.append(pl.BlockSpec((None, blk_rows, blk_cols),
                                      lambda i, j, s, last=last: (0, i, jnp.minimum(j, last))))
        out_shape.append(jax.ShapeDtypeStruct((1, k_dim, cols), BF16))
        cast_bytes += _nbytes((blk_rows, blk_cols), F32) + _nbytes((blk_rows, blk_cols), BF16)
    grid_spec = pltpu.PrefetchScalarGridSpec(
        num_scalar_prefetch=1, grid=(n_rows, n_tiles), in_specs=in_specs, out_specs=out_specs)
    outs = pl.pallas_call(
        _inproj_kernel,
        grid_spec=grid_spec,
        out_shape=out_shape,
        compiler_params=pltpu.CompilerParams(
            dimension_semantics=("arbitrary", "arbitrary"),
            vmem_limit_bytes=_vmem_limit(_nbytes((tm, D_MODEL), BF16), _nbytes((D_MODEL, PROJ_TILE), BF16),
                                         _nbytes((tm, LANES), F32), _nbytes((tm, PROJ_TILE), BF16), cast_bytes)),
        name="inproj",
    )(src, xg, w_bf, r, *[w for w, _ in cast_jobs])
    return outs[0], outs[1:]


def _slab_pitch(rows):
    p = -(-rows // SUBLANES)
    if p % 2 == 0:
        p += 1
    return p * SUBLANES


def _conv_kernel(a_ref, b_ref, za_ref, ah_ref, bh_ref, dw_ref, lng_ref, lnb_ref, pw_ref, o_ref,
                 u_scr, y_scr, s_scr, *, u_pitch, y_pitch):
    i = pl.program_id(1)
    ts = a_ref.shape[1]
    n_col = CONV_WIDTH // LANES
    for c in range(n_col):
        lanes = slice(c * LANES, (c + 1) * LANES)
        uh = ah_ref[0, :, lanes].astype(F32) * jax.nn.sigmoid(bh_ref[0, :, lanes].astype(F32))
        u_scr[c * u_pitch:c * u_pitch + CONV_HALO, :] = jnp.where(i > 0, uh, 0.0)
        u_scr[c * u_pitch + CONV_HALO:c * u_pitch + CONV_HALO + ts, :] = (
            a_ref[0, :, lanes].astype(F32) * jax.nn.sigmoid(b_ref[0, :, lanes].astype(F32)))

    first_tap = CONV_HALO - (CONV_KERNEL - 1)

    def conv_chunk(ci, carry):
        t0 = ci * CONV_STEPS
        taps_in = [u_scr[pl.ds(t0 + first_tap + k, n_col, stride=u_pitch), :]
                   for k in range(CONV_STEPS + CONV_KERNEL - 1)]
        for r in range(CONV_STEPS):
            acc = dw_ref[0] * taps_in[r]
            for j in range(1, CONV_KERNEL):
                acc = acc + dw_ref[j] * taps_in[r + j]
            y_scr[pl.ds(t0 + r, n_col, stride=y_pitch), :] = acc
        return carry

    lax.fori_loop(0, ts // CONV_STEPS, conv_chunk, 0)

    lng = lng_ref[...]
    lnb = lnb_ref[...]

    for ci in range(ts // BLOCK):
        parts = []
        for sub in range(BLOCK // NORM_ROWS):
            r0 = ci * BLOCK + sub * NORM_ROWS
            y = jnp.concatenate([y_scr[c * y_pitch + r0:c * y_pitch + r0 + NORM_ROWS, :] for c in range(n_col)],
                                axis=1)
            mu = jnp.mean(y, axis=-1, keepdims=True)
            d = y - mu
            var = jnp.mean(d * d, axis=-1, keepdims=True)
            yn = d * lax.rsqrt(var + LN_EPS) * lng + lnb
            parts.append(_silu(yn).astype(BF16))
        rows = slice(ci * BLOCK, (ci + 1) * BLOCK)
        y = jnp.dot(jnp.concatenate(parts, axis=0), pw_ref[...], preferred_element_type=F32)
        o_ref[0, rows, :] = (y * _silu(za_ref[0, rows, :].astype(F32))).astype(o_ref.dtype)


def _conv_branch(proj3, dw, ln_g, ln_b, pw_bf, *, ts=1024):
    b, s, _ = proj3.shape
    halo_per_blk = ts // CONV_HALO
    n_col = CONV_WIDTH // LANES
    u_pitch = _slab_pitch(CONV_HALO + ts)
    y_pitch = _slab_pitch(ts)
    cur = lambda blk: pl.BlockSpec((1, ts, CONV_WIDTH), lambda bi, i: (bi, i, blk))
    halo = lambda blk: pl.BlockSpec((1, CONV_HALO, CONV_WIDTH),
                                    lambda bi, i: (bi, jnp.maximum(i * halo_per_blk - 1, 0), blk))
    full = lambda shape: pl.BlockSpec(shape, lambda bi, i: (0,) * len(shape))
    scratch = [pltpu.VMEM((n_col * u_pitch, LANES), F32),
               pltpu.VMEM((n_col * y_pitch, LANES), F32),
               pltpu.VMEM((ts, CONV_WIDTH), BF16)]
    return pl.pallas_call(
        functools.partial(_conv_kernel, u_pitch=u_pitch, y_pitch=y_pitch),
        grid=(b, s // ts),
        in_specs=[cur(A_BLK1024), cur(B_BLK1024), cur(ZA_BLK1024), halo(A_BLK1024), halo(B_BLK1024),
                  full((CONV_KERNEL, n_col, LANES)), full((1, CONV_WIDTH)), full((1, CONV_WIDTH)),
                  full((CONV_WIDTH, CONV_WIDTH))],
        out_specs=pl.BlockSpec((1, ts, CONV_WIDTH), lambda bi, i: (bi, i, 0)),
        out_shape=jax.ShapeDtypeStruct((b, s, CONV_WIDTH), BF16),
        scratch_shapes=scratch,
        compiler_params=pltpu.CompilerParams(
            dimension_semantics=("parallel", "arbitrary"),
            vmem_limit_bytes=_vmem_limit(4 * _nbytes((ts, CONV_WIDTH), BF16),
                                         2 * _nbytes((CONV_HALO, CONV_WIDTH), BF16),
                                         _nbytes((CONV_WIDTH, CONV_WIDTH), BF16),
                                         _nbytes((CONV_KERNEL + 2, CONV_WIDTH), F32),
                                         scratch=sum(_nbytes(sc.shape, sc.dtype) for sc in scratch)
                                         + 2 * _nbytes((ts, CONV_WIDTH), F32))),
        name="conv_branch",
    )(proj3, proj3, proj3, proj3, proj3, dw.reshape(CONV_KERNEL, n_col, LANES),
      ln_g.reshape(1, -1), ln_b.reshape(1, -1), pw_bf)


def _attn_kernel(sink_ref, q_ref, zb_ref, kvc_ref, kvp_ref, o_ref):
    i = pl.program_id(1)
    tq = q_ref.shape[1]
    row = lax.broadcasted_iota(jnp.int32, (BLOCK, 2 * BLOCK), 0)
    col = lax.broadcasted_iota(jnp.int32, (BLOCK, 2 * BLOCK), 1)
    rel = row + BLOCK - col
    band = (rel >= 0) & (rel <= WINDOW)
    low = lax.broadcasted_iota(jnp.int32, (2 * BLOCK, LANES), 1) < HEAD_DIM
    low_q = lax.broadcasted_iota(jnp.int32, (BLOCK, LANES), 1) < HEAD_DIM
    ones_top = jnp.where(low, 1.0, 0.0).astype(BF16)
    ones_bot = jnp.where(low, 0.0, 1.0).astype(BF16)
    v_right = jnp.concatenate([ones_top, ones_bot], axis=0)

    for qb in range(tq // BLOCK):
        rows = slice(qb * BLOCK, (qb + 1) * BLOCK)
        if qb == 0:
            kv_prev = kvp_ref[0]
            valid = band & (col >= jnp.where(i == 0, BLOCK, 0))
        else:
            kv_prev = kvc_ref[0, (qb - 1) * BLOCK:qb * BLOCK, :]
            valid = band
        bias = jnp.where(valid, 0.0, -jnp.inf).astype(F32)
        bias2 = jnp.concatenate([bias, bias], axis=1)
        kv =jnp.concatenate([kv_prev, kvc_ref[0, rows, :]], axis=0).astype(F32)

        for h in range(N_KV_HEADS):
            def pair_layout(tile):
                swapped = pltpu.roll(tile, HEAD_DIM, axis=1)
                lo_src, hi_src = (tile, swapped) if h % 2 == 0 else (swapped, tile)
                return jnp.concatenate([jnp.where(low, lo_src, 0.0), jnp.where(low, 0.0, hi_src)], axis=0)

            k_tile = kv[:, (h // 2) * LANES:(h // 2 + 1) * LANES]
            v_tile = kv[:, KV_WIDTH + (h // 2) * LANES:KV_WIDTH + (h // 2 + 1) * LANES]
            k_bd = (pair_layout(k_tile) * (HEAD_DIM ** -0.5 * LOG2E)).astype(BF16)
            v_bd = jnp.concatenate([pair_layout(v_tile).astype(BF16), v_right], axis=1)

            for p in range(Q_PER_KV // 2):
                head0 = h * Q_PER_KV + 2 * p
                lanes = slice(head0 * HEAD_DIM, (head0 + 2) * HEAD_DIM)
                s = lax.dot_general(q_ref[0, rows, lanes], k_bd, (((1,), (1,)), ((), ())),
                                    preferred_element_type=F32) + bias2
                m0 = jnp.max(s[:, :2 * BLOCK], axis=-1, keepdims=True)
                m1 = jnp.max(s[:, 2 * BLOCK:], axis=-1, keepdims=True)
                pexp = jnp.concatenate([jnp.exp2(s[:, :2 * BLOCK] - m0), jnp.exp2(s[:, 2 * BLOCK:] - m1)],
                                       axis=1).astype(BF16)
                o_ext = jnp.dot(pexp, v_bd, preferred_element_type=F32)
                sink2 = jnp.where(low_q[0:1], sink_ref[head0] * LOG2E, sink_ref[head0 + 1] * LOG2E)
                sink_term = jnp.exp2(sink2 - jnp.where(low_q, m0, m1))
                attn = o_ext[:, :LANES] / (o_ext[:, LANES:] + sink_term)
                gate = _silu(zb_ref[0, rows, lanes].astype(F32))
                o_ref[0, rows, lanes] = (attn * gate).astype(o_ref.dtype)


def _attn_branch(proj3, sinks, *, tq=512):
    b, s, _ = proj3.shape
    blocks_per_step = tq // BLOCK
    return pl.pallas_call(
        _attn_kernel,
        grid=(b, s // tq),
        in_specs=[pl.BlockSpec(memory_space=pltpu.SMEM),
                  pl.BlockSpec((1, tq, ATTN_WIDTH), lambda bi, i: (bi, i, Q_BLK2048)),
                  pl.BlockSpec((1, tq, ATTN_WIDTH), lambda bi, i: (bi, i, ZB_BLK2048)),
                  pl.BlockSpec((1, tq, 2 * KV_WIDTH), lambda bi, i: (bi, i, KV_BLK512)),
                  pl.BlockSpec((1, BLOCK, 2 * KV_WIDTH),
                               lambda bi, i: (bi, jnp.maximum(i * blocks_per_step - 1, 0), KV_BLK512))],
        out_specs=pl.BlockSpec((1, tq, ATTN_WIDTH), lambda bi, i: (bi, i, 0)),
        out_shape=jax.ShapeDtypeStruct((b, s, ATTN_WIDTH), BF16),
        compiler_params=pltpu.CompilerParams(
            dimension_semantics=("parallel", "arbitrary"),
            vmem_limit_bytes=_vmem_limit(3 * _nbytes((tq, ATTN_WIDTH), BF16),
                                         2 * _nbytes((tq, 2 * KV_WIDTH), BF16),
                                         scratch=16 * _nbytes((BLOCK, 4 * BLOCK), F32))),
        name="attn_branch",
    )(sinks, proj3, proj3, proj3, proj3)


def _pool_kernel(u_ref, z_ref, uh_ref, w_ref, sc_ref, o_ref, u_scr):
    i = pl.program_id(1)
    ts = u_ref.shape[1]
    u_scr[0:POOL_HALO, :] = jnp.where(i > 0, uh_ref[0], jnp.zeros_like(uh_ref[0]))
    u_scr[POOL_HALO:POOL_HALO + ts, :] = u_ref[0]
    k_rows = BLOCK + POOL_HALO
    rel = (lax.broadcasted_iota(jnp.int32, (BLOCK, k_rows), 0) + POOL_HALO
           - lax.broadcasted_iota(jnp.int32, (BLOCK, k_rows), 1))
    pos0 = lax.broadcasted_iota(jnp.int32, (BLOCK, POOL_GROUP), 0) + (i * ts + 1)
    for g, win in enumerate(POOL_WINDOWS):
        cols = slice(g * POOL_GROUP, (g + 1) * POOL_GROUP)
        band = jnp.where((rel >= 0) & (rel < win), 1.0, 0.0).astype(BF16)
        mixed = []
        for rb in range(ts // BLOCK):
            tot = jnp.dot(band, u_scr[rb * BLOCK:rb * BLOCK + k_rows, cols], preferred_element_type=F32)
            cur = u_ref[0, rb * BLOCK:(rb + 1) * BLOCK, cols].astype(F32)
            cnt = jnp.minimum((pos0 + rb * BLOCK).astype(F32), float(win))
            mixed.append((tot / cnt - cur).astype(BF16))
        y = jnp.dot(jnp.concatenate(mixed, axis=0), w_ref[g], preferred_element_type=F32)
        gate = _silu(z_ref[0, :, cols].astype(F32))
        o_ref[0, :, cols] = (y * sc_ref[:, cols] * gate).astype(o_ref.dtype)


def _pool_branch(proj3, pool_w_bf, pool_scale, *, ts=1024):
    b, s, _ = proj3.shape
    halo_per_blk = ts // POOL_HALO
    return pl.pallas_call(
        _pool_kernel,
        grid=(b, s // ts),
        in_specs=[pl.BlockSpec((1, ts, POOL_WIDTH), lambda bi, i: (bi, i, UC_BLK1024)),
                  pl.BlockSpec((1, ts, POOL_WIDTH), lambda bi, i: (bi, i, ZC_BLK1024)),
                  pl.BlockSpec((1, POOL_HALO, POOL_WIDTH),
                               lambda bi, i: (bi, jnp.maximum(i * halo_per_blk - 1, 0), UC_BLK1024)),
                  pl.BlockSpec((len(POOL_WINDOWS), POOL_GROUP, POOL_GROUP), lambda bi, i: (0, 0, 0)),
                  pl.BlockSpec((1, POOL_WIDTH), lambda bi, i: (0, 0))],
        out_specs=pl.BlockSpec((1, ts, POOL_WIDTH), lambda bi, i: (bi, i, 0)),
        out_shape=jax.ShapeDtypeStruct((b, s, POOL_WIDTH), BF16),
        scratch_shapes=[pltpu.VMEM((POOL_HALO + ts, POOL_WIDTH), BF16)],
        compiler_params=pltpu.CompilerParams(
            dimension_semantics=("parallel", "arbitrary"),
            vmem_limit_bytes=_vmem_limit(3 * _nbytes((ts, POOL_WIDTH), BF16),
                                         scratch=_nbytes((POOL_HALO + ts, POOL_WIDTH), BF16)
                                         + 4 * _nbytes((ts, POOL_WIDTH), F32))),
        name="pool_branch",
    )(proj3, proj3, proj3, pool_w_bf, pool_scale.reshape(1, -1))


def _outproj_kernel(ya_ref, yb_ref, yc_ref, w_ref, x_ref, g_ref, *rest, final):
    i = pl.program_id(0)
    j = pl.program_id(1)
    tn = w_ref.shape[1]
    nj = D_MODEL // tn
    if final:
        o_ref, ssq_scr, xn_scr, r_scr = rest
    else:
        xn_ref, xg_ref, r_ref, ssq_scr = rest

    def row_scale():
        ssq = jnp.sum(ssq_scr[...], axis=-1, keepdims=True)
        return lax.rsqrt(ssq * (1.0 / D_MODEL) + NORM_EPS)

    def matmul_step():
        acc = jnp.dot(ya_ref[...], w_ref[0:CONV_WIDTH, :], preferred_element_type=F32)
        acc = acc + jnp.dot(yb_ref[...], w_ref[CONV_WIDTH:CONV_WIDTH + ATTN_WIDTH, :],
                            preferred_element_type=F32)
        acc = acc + jnp.dot(yc_ref[...], w_ref[CONV_WIDTH + ATTN_WIDTH:, :], preferred_element_type=F32)
        xn = x_ref[...] + acc
        sq = xn * xn
        part = sq[:, 0:LANES]
        for c in range(1, tn // LANES):
            part = part + sq[:, c * LANES:(c + 1) * LANES]

        @pl.when(j == 0)
        def _():
            ssq_scr[...] = part

        @pl.when(j > 0)
        def _():
            ssq_scr[...] = ssq_scr[...] + part

        if final:
            xn_scr[j] = xn

            @pl.when(j == nj - 1)
            def _():
                r_scr[...] = jnp.broadcast_to(row_scale(), r_scr.shape)
        else:
            xn_ref[...] = xn
            xg_ref[...] = (xn * g_ref[...]).astype(BF16)

            @pl.when(j == nj - 1)
            def _():
                r_ref[...] = jnp.broadcast_to(row_scale(), r_ref.shape)

    if final:
        @pl.when(i > 0)
        def _():
            r = jnp.tile(r_scr[...], (1, tn // LANES))
            o_ref[...] = xn_scr[j] * r * g_ref[...]

        pl.when(i < pl.num_programs(0) - 1)(matmul_step)
    else:
        matmul_step()


def _outproj(ya, yb, yc, w_bf, layer, x2, g, *, final, tm, tn):
    t = x2.shape[0]
    nj = D_MODEL // tn
    n_rows = t // tm
    row = (lambda i: jnp.minimum(i, n_rows - 1)) if final else (lambda i: i)
    in_specs = [pl.BlockSpec((tm, CONV_WIDTH), lambda i, j: (row(i), 0)),
                pl.BlockSpec((tm, ATTN_WIDTH), lambda i, j: (row(i), 0)),
                pl.BlockSpec((tm, POOL_WIDTH), lambda i, j: (row(i), 0)),
                pl.BlockSpec((None, D_MODEL, tn), lambda i, j: (layer, 0, j)),
                pl.BlockSpec((tm, tn), lambda i, j: (row(i), j))]
    blocks = [_nbytes((tm, D_MODEL), BF16), _nbytes((D_MODEL, tn), BF16), _nbytes((tm, tn), F32)]
    scratch = [pltpu.VMEM((tm, LANES), F32)]
    if final:
        grid = (n_rows + 1, nj)
        in_specs.append(pl.BlockSpec((1, tn), lambda i, j: (0, j)))
        out_specs = pl.BlockSpec((tm, tn), lambda i, j: (jnp.maximum(i - 1, 0), jnp.where(i == 0, 0, j)))
        out_shape = jax.ShapeDtypeStruct((t, D_MODEL), F32)
        blocks.append(_nbytes((tm, tn), F32))
        scratch += [pltpu.VMEM((nj, tm, tn), F32), pltpu.VMEM((tm, LANES), F32)]
    else:
        grid = (n_rows, nj)
        in_specs.append(pl.BlockSpec((1, tn), lambda i, j: (0, j)))
        out_specs = [pl.BlockSpec((tm, tn), lambda i, j: (i, j)),
                     pl.BlockSpec((tm, tn), lambda i, j: (i, j)),
                     pl.BlockSpec((tm, LANES), lambda i, j: (i, 0))]
        out_shape = [jax.ShapeDtypeStruct((t, D_MODEL), F32),
                     jax.ShapeDtypeStruct((t, D_MODEL), BF16),
                     jax.ShapeDtypeStruct((t, LANES), F32)]
        blocks += [_nbytes((tm, tn), F32), _nbytes((tm, tn), BF16)]
    return pl.pallas_call(
        functools.partial(_outproj_kernel, final=final),
        grid=grid,
        in_specs=in_specs,
        out_specs=out_specs,
        out_shape=out_shape,
        scratch_shapes=scratch,
        compiler_params=pltpu.CompilerParams(
            dimension_semantics=("arbitrary", "arbitrary"),
            vmem_limit_bytes=_vmem_limit(*blocks, scratch=2 * _nbytes((tm, tn), F32)
                                         + sum(_nbytes(sc.shape, sc.dtype) for sc in scratch))),
        name="outproj_final" if final else "outproj",
    )(ya, yb, yc, w_bf, x2, g.reshape(1, D_MODEL))


def kernel(x, norm_g, w_in, conv_dw, conv_ln_g, conv_ln_b, conv_pw, attn_sinks, pool_w, pool_scale,
           w_out, final_norm_g):
    b, s, d = x.shape
    assert d == D_MODEL and w_in.shape[-1] == IN_WIDTH and s % 512 == 0
    depth = w_in.shape[0]
    t = b * s
    conv_pw_bf = conv_pw.astype(BF16)
    pool_w_bf = pool_w.astype(BF16)
    w_in_bf = [w_in[0:1].astype(BF16)]
    cast_jobs = [(w_in, l) for l in range(1, depth)] + [(w_out, l) for l in range(depth)]

    x2 = x.reshape(t, d)
    xg, r = _prenorm(x2, norm_g[0])
    for l in range(depth):
        proj, casted = _inproj(xg, w_in_bf[l], 0, r, cast_jobs if l == 0 else ())
        if l == 0:
            w_in_bf += list(casted[:depth - 1])
            w_out_bf = list(casted[depth - 1:])
        proj3 = proj.reshape(b, s, IN_WIDTH)
        ya = _conv_branch(proj3, conv_dw[l], conv_ln_g[l], conv_ln_b[l], conv_pw_bf[l]).reshape(t, CONV_WIDTH)
        yb = _attn_branch(proj3, attn_sinks[l]).reshape(t, ATTN_WIDTH)
        yc = _pool_branch(proj3, pool_w_bf[l], pool_scale[l]).reshape(t, POOL_WIDTH)
        if l + 1 < depth:
            x2, xg, r = _outproj(ya, yb, yc, w_out_bf[l], 0, x2, norm_g[l + 1], final=False, tm=1024, tn=512)
        else:
            x2 = _outproj(ya, yb, yc, w_out_bf[l], 0, x2, final_norm_g, final=True, tm=1024, tn=512)
    return x2.reshape(b, s, d)
```

```python
import functools

import jax
import jax.numpy as jnp
from jax import lax
from jax.experimental import pallas as pl
from jax.experimental.pallas import tpu as pltpu

F32 = jnp.float32
BF16 = jnp.bfloat16

D_MODEL = 4096
CONV_WIDTH = 1024
ATTN_WIDTH = 2048
POOL_WIDTH = 1024
HEAD_DIM = 64
N_Q_HEADS = 32
N_KV_HEADS = 4
Q_PER_KV = N_Q_HEADS // N_KV_HEADS
KV_WIDTH = N_KV_HEADS * HEAD_DIM
WINDOW = 128
BLOCK = 128
CONV_KERNEL = 31
POOL_WINDOWS = (2, 4, 8, 16)
POOL_GROUP = 256
NORM_EPS = 1e-5
LN_EPS = 1e-5
IN_WIDTH = 9728
LOG2E = 1.4426950408889634

LANES = 128
SUBLANES = 8
V7X_VMEM_BYTES = 64 * 1024 * 1024
VMEM_HEADROOM_BYTES = 6 * 1024 * 1024
VMEM_RESERVE_BYTES = 2 * 1024 * 1024

PROJ_TILE = 512
PROJ_SRC_TILE = (6, 7, 8, 9, 11, 12, 13, 14, 0, 1, 2, 3, 4, 5, 15, 16, 17, 18, 10)
Q_BLK2048, ZB_BLK2048 = 0, 1
A_BLK1024, B_BLK1024, ZA_BLK1024, UC_BLK1024, ZC_BLK1024 = 4, 5, 6, 7, 8
KV_BLK512 = 18

CONV_HALO = 32
POOL_HALO = 16
CONV_STEPS = 64
NORM_ROWS = 32
TAIL_ROWS = 512


def _vmem_limit(*block_bytes, scratch=0):
    need = 2 * sum(block_bytes) + scratch + VMEM_HEADROOM_BYTES
    return min(need, V7X_VMEM_BYTES - VMEM_RESERVE_BYTES)


def _nbytes(shape, dtype):
    n = 1
    for s in shape:
        n *= s
    return n * jnp.dtype(dtype).itemsize


def _silu(z):
    return z * jax.nn.sigmoid(z)


def _prenorm_kernel(x_ref, g_ref, xg_ref, r_ref):
    x = x_ref[...]
    ssq = jnp.sum(x * x, axis=-1, keepdims=True)
    r = lax.rsqrt(ssq * (1.0 / D_MODEL) + NORM_EPS)
    xg_ref[...] = (x * g_ref[...]).astype(BF16)
    r_ref[...] = jnp.broadcast_to(r, r_ref.shape)


def _prenorm(x2, g, *, rows=256):
    t = x2.shape[0]
    return pl.pallas_call(
        _prenorm_kernel,
        grid=(t // rows,),
        in_specs=[pl.BlockSpec((rows, D_MODEL), lambda i: (i, 0)),
                  pl.BlockSpec((1, D_MODEL), lambda i: (0, 0))],
        out_specs=[pl.BlockSpec((rows, D_MODEL), lambda i: (i, 0)),
                   pl.BlockSpec((rows, LANES), lambda i: (i, 0))],
        out_shape=[jax.ShapeDtypeStruct((t, D_MODEL), BF16),
                   jax.ShapeDtypeStruct((t, LANES), F32)],
        compiler_params=pltpu.CompilerParams(
            dimension_semantics=("parallel",),
            vmem_limit_bytes=_vmem_limit(_nbytes((rows, D_MODEL), F32), _nbytes((rows, D_MODEL), BF16))),
        name="prenorm",
    )(x2, g.reshape(1, D_MODEL))


def _inproj_kernel(src_ref, xg_ref, w_ref, r_ref, *rest):
    del src_ref
    n_cast = (len(rest) - 1) // 2
    cast_in, o_ref, cast_out = rest[:n_cast], rest[n_cast], rest[n_cast + 1:]
    acc = jnp.dot(xg_ref[...], w_ref[...], preferred_element_type=F32)
    r = jnp.tile(r_ref[...], (1, acc.shape[1] // LANES))
    o_ref[...] = (acc * r).astype(o_ref.dtype)
    for src_blk, dst_blk in zip(cast_in, cast_out):
        dst_blk[...] = src_blk[...].astype(dst_blk.dtype)


def _inproj(xg, w_bf, layer, r, cast_jobs=(), *, tm=2048):
    t = xg.shape[0]
    n_tiles = IN_WIDTH // PROJ_TILE
    n_rows = t // tm
    src = jnp.asarray(PROJ_SRC_TILE, dtype=jnp.int32)
    in_specs = [pl.BlockSpec((tm, D_MODEL), lambda i, j, s: (i, 0)),
                pl.BlockSpec((None, D_MODEL, PROJ_TILE), lambda i, j, s: (layer, 0, s[j])),
                pl.BlockSpec((tm, LANES), lambda i, j, s: (i, 0))]
    out_specs = [pl.BlockSpec((tm, PROJ_TILE), lambda i, j, s: (i, j))]
    out_shape = [jax.ShapeDtypeStruct((t, IN_WIDTH), BF16)]
    cast_bytes = 0
    for w_f32, w_layer in cast_jobs:
        _, k_dim, cols = w_f32.shape
        blk_rows = k_dim // n_rows
        blk_cols = PROJ_TILE if cols % (PROJ_TILE * n_tiles) == 0 else cols // 16
        last = cols // blk_cols - 1
        in_specs.append(pl.BlockSpec((None, blk_rows, blk_cols),
                                     lambda i, j, s, w_layer=w_layer, last=last: (w_layer, i, jnp.minimum(j, last))))
        out_specs.append(pl.BlockSpec((None, blk_rows, blk_cols),
                                      lambda i, j, s, last=last: (0, i, jnp.minimum(j, last))))
        out_shape.append(jax.ShapeDtypeStruct((1, k_dim, cols), BF16))
        cast_bytes += _nbytes((blk_rows, blk_cols), F32) + _nbytes((blk_rows, blk_cols), BF16)
    grid_spec = pltpu.PrefetchScalarGridSpec(
        num_scalar_prefetch=1, grid=(n_rows, n_tiles), in_specs=in_specs, out_specs=out_specs)
    outs = pl.pallas_call(
        _inproj_kernel,
        grid_spec=grid_spec,
        out_shape=out_shape,
        compiler_params=pltpu.CompilerParams(
            dimension_semantics=("arbitrary", "arbitrary"),
            vmem_limit_bytes=_vmem_limit(_nbytes((tm, D_MODEL), BF16), _nbytes((D_MODEL, PROJ_TILE), BF16),
                                         _nbytes((tm, LANES), F32), _nbytes((tm, PROJ_TILE), BF16), cast_bytes)),
        name="inproj",
    )(src, xg, w_bf, r, *[w for w, _ in cast_jobs])
    return outs[0], outs[1:]


def _slab_pitch(rows):
    p = -(-rows // SUBLANES)
    if p % 2 == 0:
        p += 1
    return p * SUBLANES


def _conv_kernel(a_ref, b_ref, za_ref, ah_ref, bh_ref, dw_ref, lng_ref, lnb_ref, pw_ref, o_ref,
                 u_scr, y_scr, s_scr, *, u_pitch, y_pitch):
    i = pl.program_id(1)
    ts = a_ref.shape[1]
    n_col = CONV_WIDTH // LANES
    for c in range(n_col):
        lanes = slice(c * LANES, (c + 1) * LANES)
        uh = ah_ref[0, :, lanes].astype(F32) * jax.nn.sigmoid(bh_ref[0, :, lanes].astype(F32))
        u_scr[c * u_pitch:c * u_pitch + CONV_HALO, :] = jnp.where(i > 0, uh, 0.0)
        u_scr[c * u_pitch + CONV_HALO:c * u_pitch + CONV_HALO + ts, :] = (
            a_ref[0, :, lanes].astype(F32) * jax.nn.sigmoid(b_ref[0, :, lanes].astype(F32)))

    first_tap = CONV_HALO - (CONV_KERNEL - 1)

    def conv_chunk(ci, carry):
        t0 = ci * CONV_STEPS
        taps_in = [u_scr[pl.ds(t0 + first_tap + k, n_col, stride=u_pitch), :]
                   for k in range(CONV_STEPS + CONV_KERNEL - 1)]
        for r in range(CONV_STEPS):
            acc = dw_ref[0] * taps_in[r]
            for j in range(1, CONV_KERNEL):
                acc = acc + dw_ref[j] * taps_in[r + j]
            y_scr[pl.ds(t0 + r, n_col, stride=y_pitch), :] = acc
        return carry

    lax.fori_loop(0, ts // CONV_STEPS, conv_chunk, 0)

    lng = lng_ref[...]
    lnb = lnb_ref[...]

    for ci in range(ts // TAIL_ROWS):
        parts = []
        for sub in range(TAIL_ROWS // NORM_ROWS):
            r0 = ci * TAIL_ROWS + sub * NORM_ROWS
            y = jnp.concatenate([y_scr[c * y_pitch + r0:c * y_pitch + r0 + NORM_ROWS, :] for c in range(n_col)],
                                axis=1)
            mu = jnp.mean(y, axis=-1, keepdims=True)
            d = y - mu
            var = jnp.mean(d * d, axis=-1, keepdims=True)
            yn = d * lax.rsqrt(var + LN_EPS) * lng + lnb
            parts.append(_silu(yn).astype(BF16))
        rows = slice(ci * TAIL_ROWS, (ci + 1) * TAIL_ROWS)
        y = jnp.dot(jnp.concatenate(parts, axis=0), pw_ref[...], preferred_element_type=F32)
        o_ref[0, rows, :] = (y * _silu(za_ref[0, rows, :].astype(F32))).astype(o_ref.dtype)


def _conv_branch(proj3, dw, ln_g, ln_b, pw_bf, *, ts=1024):
    b, s, _ = proj3.shape
    halo_per_blk = ts // CONV_HALO
    n_col = CONV_WIDTH // LANES
    u_pitch = _slab_pitch(CONV_HALO + ts)
    y_pitch = _slab_pitch(ts)
    cur = lambda blk: pl.BlockSpec((1, ts, CONV_WIDTH), lambda bi, i: (bi, i, blk))
    halo = lambda blk: pl.BlockSpec((1, CONV_HALO, CONV_WIDTH),
                                    lambda bi, i: (bi, jnp.maximum(i * halo_per_blk - 1, 0), blk))
    full = lambda shape: pl.BlockSpec(shape, lambda bi, i: (0,) * len(shape))
    scratch = [pltpu.VMEM((n_col * u_pitch, LANES), F32),
               pltpu.VMEM((n_col * y_pitch, LANES), F32),
               pltpu.VMEM((ts, CONV_WIDTH), BF16)]
    return pl.pallas_call(
        functools.partial(_conv_kernel, u_pitch=u_pitch, y_pitch=y_pitch),
        grid=(b, s // ts),
        in_specs=[cur(A_BLK1024), cur(B_BLK1024), cur(ZA_BLK1024), halo(A_BLK1024), halo(B_BLK1024),
                  full((CONV_KERNEL, n_col, LANES)), full((1, CONV_WIDTH)), full((1, CONV_WIDTH)),
                  full((CONV_WIDTH, CONV_WIDTH))],
        out_specs=pl.BlockSpec((1, ts, CONV_WIDTH), lambda bi, i: (bi, i, 0)),
        out_shape=jax.ShapeDtypeStruct((b, s, CONV_WIDTH), BF16),
        scratch_shapes=scratch,
        compiler_params=pltpu.CompilerParams(
            dimension_semantics=("parallel", "arbitrary"),
            vmem_limit_bytes=_vmem_limit(4 * _nbytes((ts, CONV_WIDTH), BF16),
                                         2 * _nbytes((CONV_HALO, CONV_WIDTH), BF16),
                                         _nbytes((CONV_WIDTH, CONV_WIDTH), BF16),
                                         _nbytes((CONV_KERNEL + 2, CONV_WIDTH), F32),
                                         scratch=sum(_nbytes(sc.shape, sc.dtype) for sc in scratch)
                                         + 2 * _nbytes((ts, CONV_WIDTH), F32))),
        name="conv_branch",
    )(proj3, proj3, proj3, proj3, proj3, dw.reshape(CONV_KERNEL, n_col, LANES),
      ln_g.reshape(1, -1), ln_b.reshape(1, -1), pw_bf)


def _attn_kernel(sink_ref, q_ref, zb_ref, kvc_ref, kvp_ref, o_ref):
    i = pl.program_id(1)
    tq = q_ref.shape[1]
    row = lax.broadcasted_iota(jnp.int32, (BLOCK, 2 * BLOCK), 0)
    col = lax.broadcasted_iota(jnp.int32, (BLOCK, 2 * BLOCK), 1)
    rel = row + BLOCK - col
    band = (rel >= 0) & (rel <= WINDOW)
    low = lax.broadcasted_iota(jnp.int32, (2 * BLOCK, LANES), 1) < HEAD_DIM
    low_q = lax.broadcasted_iota(jnp.int32, (BLOCK, LANES), 1) < HEAD_DIM
    ones_top = jnp.where(low, 1.0, 0.0).astype(BF16)
    ones_bot = jnp.where(low, 0.0, 1.0).astype(BF16)
    v_right = jnp.concatenate([ones_top, ones_bot], axis=0)

    for qb in range(tq // BLOCK):
        rows = slice(qb * BLOCK, (qb + 1) * BLOCK)
        if qb == 0:
            kv_prev = kvp_ref[0]
            valid = band & (col >= jnp.where(i == 0, BLOCK, 0))
        else:
            kv_prev = kvc_ref[0, (qb - 1) * BLOCK:qb * BLOCK, :]
            valid = band
        bias = jnp.where(valid, 0.0, -jnp.inf).astype(F32)
        bias2 = jnp.concatenate([bias, bias], axis=1)
        kv =jnp.concatenate([kv_prev, kvc_ref[0, rows, :]], axis=0).astype(F32)

        for h in range(N_KV_HEADS):
            def pair_layout(tile):
                swapped = pltpu.roll(tile, HEAD_DIM, axis=1)
                lo_src, hi_src = (tile, swapped) if h % 2 == 0 else (swapped, tile)
                return jnp.concatenate([jnp.where(low, lo_src, 0.0), jnp.where(low, 0.0, hi_src)], axis=0)

            k_tile = kv[:, (h // 2) * LANES:(h // 2 + 1) * LANES]
            v_tile = kv[:, KV_WIDTH + (h // 2) * LANES:KV_WIDTH + (h // 2 + 1) * LANES]
            k_bd = (pair_layout(k_tile) * (HEAD_DIM ** -0.5 * LOG2E)).astype(BF16)
            v_bd = jnp.concatenate([pair_layout(v_tile).astype(BF16), v_right], axis=1)

            for p in range(Q_PER_KV // 2):
                head0 = h * Q_PER_KV + 2 * p
                lanes = slice(head0 * HEAD_DIM, (head0 + 2) * HEAD_DIM)
                s = lax.dot_general(q_ref[0, rows, lanes], k_bd, (((1,), (1,)), ((), ())),
                                    preferred_element_type=F32) + bias2
                m0 = jnp.max(s[:, :2 * BLOCK], axis=-1, keepdims=True)
                m1 = jnp.max(s[:, 2 * BLOCK:], axis=-1, keepdims=True)
                pexp = jnp.concatenate([jnp.exp2(s[:, :2 * BLOCK] - m0), jnp.exp2(s[:, 2 * BLOCK:] - m1)],
                                       axis=1).astype(BF16)
                o_ext = jnp.dot(pexp, v_bd, preferred_element_type=F32)
                sink2 = jnp.where(low_q[0:1], sink_ref[head0] * LOG2E, sink_ref[head0 + 1] * LOG2E)
                sink_term = jnp.exp2(sink2 - jnp.where(low_q, m0, m1))
                attn = o_ext[:, :LANES] / (o_ext[:, LANES:] + sink_term)
                gate = _silu(zb_ref[0, rows, lanes].astype(F32))
                o_ref[0, rows, lanes] = (attn * gate).astype(o_ref.dtype)


def _attn_branch(proj3, sinks, *, tq=1024):
    b, s, _ = proj3.shape
    blocks_per_step = tq // BLOCK
    return pl.pallas_call(
        _attn_kernel,
        grid=(b, s // tq),
        in_specs=[pl.BlockSpec(memory_space=pltpu.SMEM),
                  pl.BlockSpec((1, tq, ATTN_WIDTH), lambda bi, i: (bi, i, Q_BLK2048)),
                  pl.BlockSpec((1, tq, ATTN_WIDTH), lambda bi, i: (bi, i, ZB_BLK2048)),
                  pl.BlockSpec((1, tq, 2 * KV_WIDTH), lambda bi, i: (bi, i, KV_BLK512)),
                  pl.BlockSpec((1, BLOCK, 2 * KV_WIDTH),
                               lambda bi, i: (bi, jnp.maximum(i * blocks_per_step - 1, 0), KV_BLK512))],
        out_specs=pl.BlockSpec((1, tq, ATTN_WIDTH), lambda bi, i: (bi, i, 0)),
        out_shape=jax.ShapeDtypeStruct((b, s, ATTN_WIDTH), BF16),
        compiler_params=pltpu.CompilerParams(
            dimension_semantics=("parallel", "arbitrary"),
            vmem_limit_bytes=_vmem_limit(3 * _nbytes((tq, ATTN_WIDTH), BF16),
                                         2 * _nbytes((tq, 2 * KV_WIDTH), BF16),
                                         scratch=16 * _nbytes((BLOCK, 4 * BLOCK), F32))),
        name="attn_branch",
    )(sinks, proj3, proj3, proj3, proj3)


def _pool_kernel(u_ref, z_ref, uh_ref, w_ref, sc_ref, o_ref, u_scr):
    i = pl.program_id(1)
    ts = u_ref.shape[1]
    u_scr[0:POOL_HALO, :] = jnp.where(i > 0, uh_ref[0], jnp.zeros_like(uh_ref[0]))
    u_scr[POOL_HALO:POOL_HALO + ts, :] = u_ref[0]
    k_rows = BLOCK + POOL_HALO
    rel = (lax.broadcasted_iota(jnp.int32, (BLOCK, k_rows), 0) + POOL_HALO
           - lax.broadcasted_iota(jnp.int32, (BLOCK, k_rows), 1))
    pos0 = lax.broadcasted_iota(jnp.int32, (BLOCK, POOL_GROUP), 0) + (i * ts + 1)
    for g, win in enumerate(POOL_WINDOWS):
        cols = slice(g * POOL_GROUP, (g + 1) * POOL_GROUP)
        band = jnp.where((rel >= 0) & (rel < win), 1.0, 0.0).astype(BF16)
        mixed = []
        for rb in range(ts // BLOCK):
            tot = jnp.dot(band, u_scr[rb * BLOCK:rb * BLOCK + k_rows, cols], preferred_element_type=F32)
            cur = u_ref[0, rb * BLOCK:(rb + 1) * BLOCK, cols].astype(F32)
            cnt = jnp.minimum((pos0 + rb * BLOCK).astype(F32), float(win))
            mixed.append((tot / cnt - cur).astype(BF16))
        y = jnp.dot(jnp.concatenate(mixed, axis=0), w_ref[g], preferred_element_type=F32)
        gate = _silu(z_ref[0, :, cols].astype(F32))
        o_ref[0, :, cols] = (y * sc_ref[:, cols] * gate).astype(o_ref.dtype)


def _pool_branch(proj3, pool_w_bf, pool_scale, *, ts=1024):
    b, s, _ = proj3.shape
    halo_per_blk = ts // POOL_HALO
    return pl.pallas_call(
        _pool_kernel,
        grid=(b, s // ts),
        in_specs=[pl.BlockSpec((1, ts, POOL_WIDTH), lambda bi, i: (bi, i, UC_BLK1024)),
                  pl.BlockSpec((1, ts, POOL_WIDTH), lambda bi, i: (bi, i, ZC_BLK1024)),
                  pl.BlockSpec((1, POOL_HALO, POOL_WIDTH),
                               lambda bi, i: (bi, jnp.maximum(i * halo_per_blk - 1, 0), UC_BLK1024)),
                  pl.BlockSpec((len(POOL_WINDOWS), POOL_GROUP, POOL_GROUP), lambda bi, i: (0, 0, 0)),
                  pl.BlockSpec((1, POOL_WIDTH), lambda bi, i: (0, 0))],
        out_specs=pl.BlockSpec((1, ts, POOL_WIDTH), lambda bi, i: (bi, i, 0)),
        out_shape=jax.ShapeDtypeStruct((b, s, POOL_WIDTH), BF16),
        scratch_shapes=[pltpu.VMEM((POOL_HALO + ts, POOL_WIDTH), BF16)],
        compiler_params=pltpu.CompilerParams(
            dimension_semantics=("parallel", "arbitrary"),
            vmem_limit_bytes=_vmem_limit(3 * _nbytes((ts, POOL_WIDTH), BF16),
                                         scratch=_nbytes((POOL_HALO + ts, POOL_WIDTH), BF16)
                                         + 4 * _nbytes((ts, POOL_WIDTH), F32))),
        name="pool_branch",
    )(proj3, proj3, proj3, pool_w_bf, pool_scale.reshape(1, -1))


def _outproj_kernel(ya_ref, yb_ref, yc_ref, w_ref, x_ref, g_ref, *rest, final):
    i = pl.program_id(0)
    j = pl.program_id(1)
    tn = w_ref.shape[1]
    nj = D_MODEL // tn
    if final:
        o_ref, ssq_scr, xn_scr, r_scr = rest
    else:
        xn_ref, xg_ref, r_ref, ssq_scr = rest

    def row_scale():
        ssq = jnp.sum(ssq_scr[...], axis=-1, keepdims=True)
        return lax.rsqrt(ssq * (1.0 / D_MODEL) + NORM_EPS)

    def matmul_step():
        acc = jnp.dot(ya_ref[...], w_ref[0:CONV_WIDTH, :], preferred_element_type=F32)
        acc = acc + jnp.dot(yb_ref[...], w_ref[CONV_WIDTH:CONV_WIDTH + ATTN_WIDTH, :],
                            preferred_element_type=F32)
        acc = acc + jnp.dot(yc_ref[...], w_ref[CONV_WIDTH + ATTN_WIDTH:, :], preferred_element_type=F32)
        xn = x_ref[...] + acc
        sq = xn * xn
        part = sq[:, 0:LANES]
        for c in range(1, tn // LANES):
            part = part + sq[:, c * LANES:(c + 1) * LANES]

        @pl.when(j == 0)
        def _():
            ssq_scr[...] = part

        @pl.when(j > 0)
        def _():
            ssq_scr[...] = ssq_scr[...] + part

        if final:
            xn_scr[j] = xn

            @pl.when(j == nj - 1)
            def _():
                r_scr[...] = jnp.broadcast_to(row_scale(), r_scr.shape)
        else:
            xn_ref[...] = xn
            xg_ref[...] = (xn * g_ref[...]).astype(BF16)

            @pl.when(j == nj - 1)
            def _():
                r_ref[...] = jnp.broadcast_to(row_scale(), r_ref.shape)

    if final:
        @pl.when(i > 0)
        def _():
            r = jnp.tile(r_scr[...], (1, tn // LANES))
            o_ref[...] = xn_scr[j] * r * g_ref[...]

        pl.when(i < pl.num_programs(0) - 1)(matmul_step)
    else:
        matmul_step()


def _outproj(ya, yb, yc, w_bf, layer, x2, g, *, final, tm, tn):
    t = x2.shape[0]
    nj = D_MODEL // tn
    n_rows = t // tm
    row = (lambda i: jnp.minimum(i, n_rows - 1)) if final else (lambda i: i)
    in_specs = [pl.BlockSpec((tm, CONV_WIDTH), lambda i, j: (row(i), 0)),
                pl.BlockSpec((tm, ATTN_WIDTH), lambda i, j: (row(i), 0)),
                pl.BlockSpec((tm, POOL_WIDTH), lambda i, j: (row(i), 0)),
                pl.BlockSpec((None, D_MODEL, tn), lambda i, j: (layer, 0, j)),
                pl.BlockSpec((tm, tn), lambda i, j: (row(i), j))]
    blocks = [_nbytes((tm, D_MODEL), BF16), _nbytes((D_MODEL, tn), BF16), _nbytes((tm, tn), F32)]
    scratch = [pltpu.VMEM((tm, LANES), F32)]
    if final:
        grid = (n_rows + 1, nj)
        in_specs.append(pl.BlockSpec((1, tn), lambda i, j: (0, j)))
        out_specs = pl.BlockSpec((tm, tn), lambda i, j: (jnp.maximum(i - 1, 0), jnp.where(i == 0, 0, j)))
        out_shape = jax.ShapeDtypeStruct((t, D_MODEL), F32)
        blocks.append(_nbytes((tm, tn), F32))
        scratch += [pltpu.VMEM((nj, tm, tn), F32), pltpu.VMEM((tm, LANES), F32)]
    else:
        grid = (n_rows, nj)
        in_specs.append(pl.BlockSpec((1, tn), lambda i, j: (0, j)))
        out_specs = [pl.BlockSpec((tm, tn), lambda i, j: (i, j)),
                     pl.BlockSpec((tm, tn), lambda i, j: (i, j)),
                     pl.BlockSpec((tm, LANES), lambda i, j: (i, 0))]
        out_shape = [jax.ShapeDtypeStruct((t, D_MODEL), F32),
                     jax.ShapeDtypeStruct((t, D_MODEL), BF16),
                     jax.ShapeDtypeStruct((t, LANES), F32)]
        blocks += [_nbytes((tm, tn), F32), _nbytes((tm, tn), BF16)]
    return pl.pallas_call(
        functools.partial(_outproj_kernel, final=final),
        grid=grid,
        in_specs=in_specs,
        out_specs=out_specs,
        out_shape=out_shape,
        scratch_shapes=scratch,
        compiler_params=pltpu.CompilerParams(
            dimension_semantics=("arbitrary", "arbitrary"),
            vmem_limit_bytes=_vmem_limit(*blocks, scratch=2 * _nbytes((tm, tn), F32)
                                         + sum(_nbytes(sc.shape, sc.dtype) for sc in scratch))),
        name="outproj_final" if final else "outproj",
    )(ya, yb, yc, w_bf, x2, g.reshape(1, D_MODEL))


def kernel(x, norm_g, w_in, conv_dw, conv_ln_g, conv_ln_b, conv_pw, attn_sinks, pool_w, pool_scale,
           w_out, final_norm_g):
    b, s, d = x.shape
    assert d == D_MODEL and w_in.shape[-1] == IN_WIDTH and s % 1024 == 0 and (b * s) % 2048 == 0
    depth = w_in.shape[0]
    t = b * s
    conv_pw_bf = conv_pw.astype(BF16)
    pool_w_bf = pool_w.astype(BF16)
    w_in_bf = [w_in[0:1].astype(BF16)]
    cast_jobs = [(w_in, l) for l in range(1, depth)] + [(w_out, l) for l in range(depth)]

    x2 = x.reshape(t, d)
    xg, r = _prenorm(x2, norm_g[0])
    for l in range(depth):
        proj, casted = _inproj(xg, w_in_bf[l], 0, r, cast_jobs if l == 0 else ())
        if l == 0:
            w_in_bf += list(casted[:depth - 1])
            w_out_bf = list(casted[depth - 1:])
        proj3 = proj.reshape(b, s, IN_WIDTH)
        ya = _conv_branch(proj3, conv_dw[l], conv_ln_g[l], conv_ln_b[l], conv_pw_bf[l]).reshape(t, CONV_WIDTH)
        yb = _attn_branch(proj3, attn_sinks[l]).reshape(t, ATTN_WIDTH)
        yc = _pool_branch(proj3, pool_w_bf[l], pool_scale[l]).reshape(t, POOL_WIDTH)
        if l + 1 < depth:
            x2, xg, r = _outproj(ya, yb, yc, w_out_bf[l], 0, x2, norm_g[l + 1], final=False, tm=1024, tn=512)
        else:
            x2 = _outproj(ya, yb, yc, w_out_bf[l], 0, x2, final_norm_g, final=True, tm=1024, tn=512)
    return x2.reshape(b, s, d)
```

```python
import functools

import jax
import jax.numpy as jnp
from jax import lax
from jax.experimental import pallas as pl
from jax.experimental.pallas import tpu as pltpu

F32 = jnp.float32
BF16 = jnp.bfloat16

D_MODEL = 4096
CONV_WIDTH = 1024
ATTN_WIDTH = 2048
POOL_WIDTH = 1024
HEAD_DIM = 64
N_Q_HEADS = 32
N_KV_HEADS = 4
Q_PER_KV = N_Q_HEADS // N_KV_HEADS
KV_WIDTH = N_KV_HEADS * HEAD_DIM
WINDOW = 128
BLOCK = 128
CONV_KERNEL = 31
POOL_WINDOWS = (2, 4, 8, 16)
POOL_GROUP = 256
NORM_EPS = 1e-5
LN_EPS = 1e-5
IN_WIDTH = 9728
LOG2E = 1.4426950408889634

LANES = 128
SUBLANES = 8
V7X_VMEM_BYTES = 64 * 1024 * 1024
VMEM_HEADROOM_BYTES = 6 * 1024 * 1024
VMEM_RESERVE_BYTES = 2 * 1024 * 1024

PROJ_TILE = 512
PROJ_SRC_TILE = (6, 7, 8, 9, 11, 12, 13, 14, 0, 1, 2, 3, 4, 5, 15, 16, 17, 18, 10)
Q_BLK2048, ZB_BLK2048 = 0, 1
A_BLK1024, B_BLK1024, ZA_BLK1024, UC_BLK1024, ZC_BLK1024 = 4, 5, 6, 7, 8
KV_BLK512 = 18

CONV_HALO = 32
POOL_HALO = 16
CONV_STEPS = 64
NORM_ROWS = 32
TAIL_ROWS = 512


def _vmem_limit(*block_bytes, scratch=0):
    need = 2 * sum(block_bytes) + scratch + VMEM_HEADROOM_BYTES
    return min(need, V7X_VMEM_BYTES - VMEM_RESERVE_BYTES)


def _nbytes(shape, dtype):
    n = 1
    for s in shape:
        n *= s
    return n * jnp.dtype(dtype).itemsize


def _silu(z):
    return z * jax.nn.sigmoid(z)


def _prenorm_kernel(x_ref, g_ref, xg_ref, r_ref):
    x = x_ref[...]
    ssq = jnp.sum(x * x, axis=-1, keepdims=True)
    r = lax.rsqrt(ssq * (1.0 / D_MODEL) + NORM_EPS)
    xg_ref[...] = (x * g_ref[...]).astype(BF16)
    r_ref[...] = jnp.broadcast_to(r, r_ref.shape)


def _prenorm(x2, g, *, rows=256):
    t = x2.shape[0]
    return pl.pallas_call(
        _prenorm_kernel,
        grid=(t // rows,),
        in_specs=[pl.BlockSpec((rows, D_MODEL), lambda i: (i, 0)),
                  pl.BlockSpec((1, D_MODEL), lambda i: (0, 0))],
        out_specs=[pl.BlockSpec((rows, D_MODEL), lambda i: (i, 0)),
                   pl.BlockSpec((rows, LANES), lambda i: (i, 0))],
        out_shape=[jax.ShapeDtypeStruct((t, D_MODEL), BF16),
                   jax.ShapeDtypeStruct((t, LANES), F32)],
        compiler_params=pltpu.CompilerParams(
            dimension_semantics=("parallel",),
            vmem_limit_bytes=_vmem_limit(_nbytes((rows, D_MODEL), F32), _nbytes((rows, D_MODEL), BF16))),
        name="prenorm",
    )(x2, g.reshape(1, D_MODEL))


def _inproj_kernel(src_ref, xg_ref, w_ref, r_ref, *rest):
    del src_ref
    n_cast = (len(rest) - 1) // 2
    cast_in, o_ref, cast_out = rest[:n_cast], rest[n_cast], rest[n_cast + 1:]
    acc = jnp.dot(xg_ref[...], w_ref[...], preferred_element_type=F32)
    r = jnp.tile(r_ref[...], (1, acc.shape[1] // LANES))
    o_ref[...] = (acc * r).astype(o_ref.dtype)
    for src_blk, dst_blk in zip(cast_in, cast_out):
        dst_blk[...] = src_blk[...].astype(dst_blk.dtype)


def _inproj(xg, w_bf, layer, r, cast_jobs=(), *, tm=2048):
    t = xg.shape[0]
    n_tiles = IN_WIDTH // PROJ_TILE
    n_rows = t // tm
    src = jnp.asarray(PROJ_SRC_TILE, dtype=jnp.int32)
    in_specs = [pl.BlockSpec((tm, D_MODEL), lambda i, j, s: (i, 0)),
                pl.BlockSpec((None, D_MODEL, PROJ_TILE), lambda i, j, s: (layer, 0, s[j])),
                pl.BlockSpec((tm, LANES), lambda i, j, s: (i, 0))]
    out_specs = [pl.BlockSpec((tm, PROJ_TILE), lambda i, j, s: (i, j))]
    out_shape = [jax.ShapeDtypeStruct((t, IN_WIDTH), BF16)]
    cast_bytes = 0
    for w_f32, w_layer in cast_jobs:
        _, k_dim, cols = w_f32.shape
        blk_rows = k_dim // n_rows
        blk_cols = PROJ_TILE if cols % (PROJ_TILE * n_tiles) == 0 else cols // 16
        last = cols // blk_cols - 1
        in_specs.append(pl.BlockSpec((None, blk_rows, blk_cols),
                                     lambda i, j, s, w_layer=w_layer, last=last: (w_layer, i, jnp.minimum(j, last))))
        out_specs.append(pl.BlockSpec((None, blk_rows, blk_cols),
                                      lambda i, j, s, last=last: (0, i, jnp.minimum(j, last))))
        out_shape.append(jax.ShapeDtypeStruct((1, k_dim, cols), BF16))
        cast_bytes += _nbytes((blk_rows, blk_cols), F32) + _nbytes((blk_rows, blk_cols), BF16)
    grid_spec = pltpu.PrefetchScalarGridSpec(
        num_scalar_prefetch=1, grid=(n_rows, n_tiles), in_specs=in_specs, out_specs=out_specs)
    outs = pl.pallas_call(
        _inproj_kernel,
        grid_spec=grid_spec,
        out_shape=out_shape,
        compiler_params=pltpu.CompilerParams(
            dimension_semantics=("arbitrary", "arbitrary"),
            vmem_limit_bytes=_vmem_limit(_nbytes((tm, D_MODEL), BF16), _nbytes((D_MODEL, PROJ_TILE), BF16),
                                         _nbytes((tm, LANES), F32), _nbytes((tm, PROJ_TILE), BF16), cast_bytes)),
        name="inproj",
    )(src, xg, w_bf, r, *[w for w, _ in cast_jobs])
    return outs[0], outs[1:]


def _slab_pitch(rows):
    p = -(-rows // SUBLANES)
    if p % 2 == 0:
        p += 1
    return p * SUBLANES


def _conv_kernel(a_ref, b_ref, za_ref, ah_ref, bh_ref, dw_ref, lng_ref, lnb_ref, pw_ref, o_ref,
                 u_scr, y_scr, s_scr, *, u_pitch, y_pitch):
    i = pl.program_id(1)
    ts = a_ref.shape[1]
    n_col = CONV_WIDTH // LANES
    for c in range(n_col):
        lanes = slice(c * LANES, (c + 1) * LANES)
        uh = ah_ref[0, :, lanes].astype(F32) * jax.nn.sigmoid(bh_ref[0, :, lanes].astype(F32))
        u_scr[c * u_pitch:c * u_pitch + CONV_HALO, :] = jnp.where(i > 0, uh, 0.0)
        u_scr[c * u_pitch + CONV_HALO:c * u_pitch + CONV_HALO + ts, :] = (
            a_ref[0, :, lanes].astype(F32) * jax.nn.sigmoid(b_ref[0, :, lanes].astype(F32)))

    first_tap = CONV_HALO - (CONV_KERNEL - 1)

    def conv_chunk(ci, carry):
        t0 = ci * CONV_STEPS
        taps_in = [u_scr[pl.ds(t0 + first_tap + k, n_col, stride=u_pitch), :]
                   for k in range(CONV_STEPS + CONV_KERNEL - 1)]
        for r in range(CONV_STEPS):
            acc = dw_ref[0] * taps_in[r]
            for j in range(1, CONV_KERNEL):
                acc = acc + dw_ref[j] * taps_in[r + j]
            y_scr[pl.ds(t0 + r, n_col, stride=y_pitch), :] = acc
        return carry

    lax.fori_loop(0, ts // CONV_STEPS, conv_chunk, 0)

    lng = lng_ref[...]
    lnb = lnb_ref[...]

    for ci in range(ts // TAIL_ROWS):
        parts = []
        for sub in range(TAIL_ROWS // NORM_ROWS):
            r0 = ci * TAIL_ROWS + sub * NORM_ROWS
            y = jnp.concatenate([y_scr[c * y_pitch + r0:c * y_pitch + r0 + NORM_ROWS, :] for c in range(n_col)],
                                axis=1)
            mu = jnp.mean(y, axis=-1, keepdims=True)
            d = y - mu
            var = jnp.mean(d * d, axis=-1, keepdims=True)
            yn = d * lax.rsqrt(var + LN_EPS) * lng + lnb
            parts.append(_silu(yn).astype(BF16))
        rows = slice(ci * TAIL_ROWS, (ci + 1) * TAIL_ROWS)
        y = jnp.dot(jnp.concatenate(parts, axis=0), pw_ref[...], preferred_element_type=F32)
        o_ref[0, rows, :] = (y * _silu(za_ref[0, rows, :].astype(F32))).astype(o_ref.dtype)


def _attn_kernel(sink_ref, q_ref, zb_ref, kvc_ref, kvp_ref, o_ref):
    i = pl.program_id(1)
    tq = q_ref.shape[1]
    row = lax.broadcasted_iota(jnp.int32, (BLOCK, 2 * BLOCK), 0)
    col = lax.broadcasted_iota(jnp.int32, (BLOCK, 2 * BLOCK), 1)
    rel = row + BLOCK - col
    band = (rel >= 0) & (rel <= WINDOW)
    low = lax.broadcasted_iota(jnp.int32, (2 * BLOCK, LANES), 1) < HEAD_DIM
    low_q = lax.broadcasted_iota(jnp.int32, (BLOCK, LANES), 1) < HEAD_DIM
    ones_top = jnp.where(low, 1.0, 0.0).astype(BF16)
    ones_bot = jnp.where(low, 0.0, 1.0).astype(BF16)
    v_right = jnp.concatenate([ones_top, ones_bot], axis=0)

    for qb in range(tq // BLOCK):
        rows = slice(qb * BLOCK, (qb + 1) * BLOCK)
        if qb == 0:
            kv_prev = kvp_ref[0]
            valid = band & (col >= jnp.where(i == 0, BLOCK, 0))
        else:
            kv_prev = kvc_ref[0, (qb - 1) * BLOCK:qb * BLOCK, :]
            valid = band
        bias = jnp.where(valid, 0.0, -jnp.inf).astype(F32)
        bias2 = jnp.concatenate([bias, bias], axis=1)
        kv =jnp.concatenate([kv_prev, kvc_ref[0, rows, :]], axis=0).astype(F32)

        for h in range(N_KV_HEADS):
            def pair_layout(tile):
                swapped = pltpu.roll(tile, HEAD_DIM, axis=1)
                lo_src, hi_src = (tile, swapped) if h % 2 == 0 else (swapped, tile)
                return jnp.concatenate([jnp.where(low, lo_src, 0.0), jnp.where(low, 0.0, hi_src)], axis=0)

            k_tile = kv[:, (h // 2) * LANES:(h // 2 + 1) * LANES]
            v_tile = kv[:, KV_WIDTH + (h // 2) * LANES:KV_WIDTH + (h // 2 + 1) * LANES]
            k_bd = (pair_layout(k_tile) * (HEAD_DIM ** -0.5 * LOG2E)).astype(BF16)
            v_bd = jnp.concatenate([pair_layout(v_tile).astype(BF16), v_right], axis=1)

            for p in range(Q_PER_KV // 2):
                head0 = h * Q_PER_KV + 2 * p
                lanes = slice(head0 * HEAD_DIM, (head0 + 2) * HEAD_DIM)
                s = lax.dot_general(q_ref[0, rows, lanes], k_bd, (((1,), (1,)), ((), ())),
                                    preferred_element_type=F32) + bias2
                m0 = jnp.max(s[:, :2 * BLOCK], axis=-1, keepdims=True)
                m1 = jnp.max(s[:, 2 * BLOCK:], axis=-1, keepdims=True)
                pexp = jnp.concatenate([jnp.exp2(s[:, :2 * BLOCK] - m0), jnp.exp2(s[:, 2 * BLOCK:] - m1)],
                                       axis=1).astype(BF16)
                o_ext = jnp.dot(pexp, v_bd, preferred_element_type=F32)
                sink2 = jnp.where(low_q[0:1], sink_ref[head0] * LOG2E, sink_ref[head0 + 1] * LOG2E)
                sink_term = jnp.exp2(sink2 - jnp.where(low_q, m0, m1))
                attn = o_ext[:, :LANES] / (o_ext[:, LANES:] + sink_term)
                gate = _silu(zb_ref[0, rows, lanes].astype(F32))
                o_ref[0, rows, lanes] = (attn * gate).astype(o_ref.dtype)


def _attn_branch(proj3, sinks, *, tq=1024):
    b, s, _ = proj3.shape
    blocks_per_step = tq // BLOCK
    return pl.pallas_call(
        _attn_kernel,
        grid=(b, s // tq),
        in_specs=[pl.BlockSpec(memory_space=pltpu.SMEM),
                  pl.BlockSpec((1, tq, ATTN_WIDTH), lambda bi, i: (bi, i, Q_BLK2048)),
                  pl.BlockSpec((1, tq, ATTN_WIDTH), lambda bi, i: (bi, i, ZB_BLK2048)),
                  pl.BlockSpec((1, tq, 2 * KV_WIDTH), lambda bi, i: (bi, i, KV_BLK512)),
                  pl.BlockSpec((1, BLOCK, 2 * KV_WIDTH),
                               lambda bi, i: (bi, jnp.maximum(i * blocks_per_step - 1, 0), KV_BLK512))],
        out_specs=pl.BlockSpec((1, tq, ATTN_WIDTH), lambda bi, i: (bi, i, 0)),
        out_shape=jax.ShapeDtypeStruct((b, s, ATTN_WIDTH), BF16),
        compiler_params=pltpu.CompilerParams(
            dimension_semantics=("parallel", "arbitrary"),
            vmem_limit_bytes=_vmem_limit(3 * _nbytes((tq, ATTN_WIDTH), BF16),
                                         2 * _nbytes((tq, 2 * KV_WIDTH), BF16),
                                         scratch=16 * _nbytes((BLOCK, 4 * BLOCK), F32))),
        name="attn_branch",
    )(sinks, proj3, proj3, proj3, proj3)


def _pool_kernel(u_ref, z_ref, uh_ref, w_ref, sc_ref, o_ref, u_scr):
    i = pl.program_id(1)
    ts = u_ref.shape[1]
    u_scr[0:POOL_HALO, :] = jnp.where(i > 0, uh_ref[0], jnp.zeros_like(uh_ref[0]))
    u_scr[POOL_HALO:POOL_HALO + ts, :] = u_ref[0]
    k_rows = BLOCK + POOL_HALO
    rel = (lax.broadcasted_iota(jnp.int32, (BLOCK, k_rows), 0) + POOL_HALO
           - lax.broadcasted_iota(jnp.int32, (BLOCK, k_rows), 1))
    pos0 = lax.broadcasted_iota(jnp.int32, (BLOCK, POOL_GROUP), 0) + (i * ts + 1)
    for g, win in enumerate(POOL_WINDOWS):
        cols = slice(g * POOL_GROUP, (g + 1) * POOL_GROUP)
        band = jnp.where((rel >= 0) & (rel < win), 1.0, 0.0).astype(BF16)
        mixed = []
        for rb in range(ts // BLOCK):
            tot = jnp.dot(band, u_scr[rb * BLOCK:rb * BLOCK + k_rows, cols], preferred_element_type=F32)
            cur = u_ref[0, rb * BLOCK:(rb + 1) * BLOCK, cols].astype(F32)
            cnt = jnp.minimum((pos0 + rb * BLOCK).astype(F32), float(win))
            mixed.append((tot / cnt - cur).astype(BF16))
        y = jnp.dot(jnp.concatenate(mixed, axis=0), w_ref[g], preferred_element_type=F32)
        gate = _silu(z_ref[0, :, cols].astype(F32))
        o_ref[0, :, cols] = (y * sc_ref[:, cols] * gate).astype(o_ref.dtype)


N_CONV_IN, N_POOL_IN = 9, 5


def _conv_pool_kernel(*refs, u_pitch, y_pitch):
    conv_in, pool_in = refs[:N_CONV_IN], refs[N_CONV_IN:N_CONV_IN + N_POOL_IN]
    ya_ref, yc_ref, u_scr, y_scr, s_scr, pool_scr = refs[N_CONV_IN + N_POOL_IN:]
    _conv_kernel(*conv_in, ya_ref, u_scr, y_scr, s_scr, u_pitch=u_pitch, y_pitch=y_pitch)
    _pool_kernel(*pool_in, yc_ref, pool_scr)


def _conv_pool_branch(proj3, dw, ln_g, ln_b, pw_bf, pool_w_bf, pool_scale, *, ts=1024):
    b, s, _ = proj3.shape
    n_col = CONV_WIDTH // LANES
    u_pitch = _slab_pitch(CONV_HALO + ts)
    y_pitch = _slab_pitch(ts)
    cur = lambda blk: pl.BlockSpec((1, ts, CONV_WIDTH), lambda bi, i: (bi, i, blk))
    halo = lambda rows, blk: pl.BlockSpec((1, rows, CONV_WIDTH),
                                          lambda bi, i: (bi, jnp.maximum(i * (ts // rows) - 1, 0), blk))
    full = lambda shape: pl.BlockSpec(shape, lambda bi, i: (0,) * len(shape))
    in_specs = [cur(A_BLK1024), cur(B_BLK1024), cur(ZA_BLK1024), halo(CONV_HALO, A_BLK1024),
                halo(CONV_HALO, B_BLK1024), full((CONV_KERNEL, n_col, LANES)), full((1, CONV_WIDTH)),
                full((1, CONV_WIDTH)), full((CONV_WIDTH, CONV_WIDTH)),
                cur(UC_BLK1024), cur(ZC_BLK1024), halo(POOL_HALO, UC_BLK1024),
                full((len(POOL_WINDOWS), POOL_GROUP, POOL_GROUP)), full((1, POOL_WIDTH))]
    assert len(in_specs) == N_CONV_IN + N_POOL_IN and CONV_WIDTH == POOL_WIDTH
    out_blk = pl.BlockSpec((1, ts, CONV_WIDTH), lambda bi, i: (bi, i, 0))
    scratch = [pltpu.VMEM((n_col * u_pitch, LANES), F32),
               pltpu.VMEM((n_col * y_pitch, LANES), F32),
               pltpu.VMEM((ts, CONV_WIDTH), BF16),
               pltpu.VMEM((POOL_HALO + ts, POOL_WIDTH), BF16)]
    return pl.pallas_call(
        functools.partial(_conv_pool_kernel, u_pitch=u_pitch, y_pitch=y_pitch),
        grid=(b, s // ts),
        in_specs=in_specs,
        out_specs=[out_blk, out_blk],
        out_shape=[jax.ShapeDtypeStruct((b, s, CONV_WIDTH), BF16),
                   jax.ShapeDtypeStruct((b, s, POOL_WIDTH), BF16)],
        scratch_shapes=scratch,
        compiler_params=pltpu.CompilerParams(
            dimension_semantics=("parallel", "arbitrary"),
            vmem_limit_bytes=_vmem_limit(7 * _nbytes((ts, CONV_WIDTH), BF16),
                                         3 * _nbytes((CONV_HALO, CONV_WIDTH), BF16),
                                         _nbytes((CONV_WIDTH, CONV_WIDTH), BF16),
                                         _nbytes((CONV_KERNEL + 2, CONV_WIDTH), F32),
                                         scratch=sum(_nbytes(sc.shape, sc.dtype) for sc in scratch)
                                         + 4 * _nbytes((ts, CONV_WIDTH), F32))),
        name="conv_pool_branch",
    )(proj3, proj3, proj3, proj3, proj3, dw.reshape(CONV_KERNEL, n_col, LANES),
      ln_g.reshape(1, -1), ln_b.reshape(1, -1), pw_bf,
      proj3, proj3, proj3, pool_w_bf, pool_scale.reshape(1, -1))


def _outproj_kernel(ya_ref, yb_ref, yc_ref, w_ref, x_ref, g_ref, *rest, final):
    i = pl.program_id(0)
    j = pl.program_id(1)
    tn = w_ref.shape[1]
    nj = D_MODEL // tn
    if final:
        o_ref, ssq_scr, xn_scr, r_scr = rest
    else:
        xn_ref, xg_ref, r_ref, ssq_scr = rest

    def row_scale():
        ssq = jnp.sum(ssq_scr[...], axis=-1, keepdims=True)
        return lax.rsqrt(ssq * (1.0 / D_MODEL) + NORM_EPS)

    def matmul_step():
        acc = jnp.dot(ya_ref[...], w_ref[0:CONV_WIDTH, :], preferred_element_type=F32)
        acc = acc + jnp.dot(yb_ref[...], w_ref[CONV_WIDTH:CONV_WIDTH + ATTN_WIDTH, :],
                            preferred_element_type=F32)
        acc = acc + jnp.dot(yc_ref[...], w_ref[CONV_WIDTH + ATTN_WIDTH:, :], preferred_element_type=F32)
        xn = x_ref[...] + acc
        sq = xn * xn
        part = sq[:, 0:LANES]
        for c in range(1, tn // LANES):
            part = part + sq[:, c * LANES:(c + 1) * LANES]

        @pl.when(j == 0)
        def _():
            ssq_scr[...] = part

        @pl.when(j > 0)
        def _():
            ssq_scr[...] = ssq_scr[...] + part

        if final:
            xn_scr[j] = xn

            @pl.when(j == nj - 1)
            def _():
                r_scr[...] = jnp.broadcast_to(row_scale(), r_scr.shape)
        else:
            xn_ref[...] = xn
            xg_ref[...] = (xn * g_ref[...]).astype(BF16)

            @pl.when(j == nj - 1)
            def _():
                r_ref[...] = jnp.broadcast_to(row_scale(), r_ref.shape)

    if final:
        @pl.when(i > 0)
        def _():
            r = jnp.tile(r_scr[...], (1, tn // LANES))
            o_ref[...] = xn_scr[j] * r * g_ref[...]

        pl.when(i < pl.num_programs(0) - 1)(matmul_step)
    else:
        matmul_step()


def _outproj(ya, yb, yc, w_bf, layer, x2, g, *, final, tm, tn):
    t = x2.shape[0]
    nj = D_MODEL // tn
    n_rows = t // tm
    row = (lambda i: jnp.minimum(i, n_rows - 1)) if final else (lambda i: i)
    in_specs = [pl.BlockSpec((tm, CONV_WIDTH), lambda i, j: (row(i), 0)),
                pl.BlockSpec((tm, ATTN_WIDTH), lambda i, j: (row(i), 0)),
                pl.BlockSpec((tm, POOL_WIDTH), lambda i, j: (row(i), 0)),
                pl.BlockSpec((None, D_MODEL, tn), lambda i, j: (layer, 0, j)),
                pl.BlockSpec((tm, tn), lambda i, j: (row(i), j))]
    blocks = [_nbytes((tm, D_MODEL), BF16), _nbytes((D_MODEL, tn), BF16), _nbytes((tm, tn), F32)]
    scratch = [pltpu.VMEM((tm, LANES), F32)]
    if final:
        grid = (n_rows + 1, nj)
        in_specs.append(pl.BlockSpec((1, tn), lambda i, j: (0, j)))
        out_specs = pl.BlockSpec((tm, tn), lambda i, j: (jnp.maximum(i - 1, 0), jnp.where(i == 0, 0, j)))
        out_shape = jax.ShapeDtypeStruct((t, D_MODEL), F32)
        blocks.append(_nbytes((tm, tn), F32))
        scratch += [pltpu.VMEM((nj, tm, tn), F32), pltpu.VMEM((tm, LANES), F32)]
    else:
        grid = (n_rows, nj)
        in_specs.append(pl.BlockSpec((1, tn), lambda i, j: (0, j)))
        out_specs = [pl.BlockSpec((tm, tn), lambda i, j: (i, j)),
                     pl.BlockSpec((tm, tn), lambda i, j: (i, j)),
                     pl.BlockSpec((tm, LANES), lambda i, j: (i, 0))]
        out_shape = [jax.ShapeDtypeStruct((t, D_MODEL), F32),
                     jax.ShapeDtypeStruct((t, D_MODEL), BF16),
                     jax.ShapeDtypeStruct((t, LANES), F32)]
        blocks += [_nbytes((tm, tn), F32), _nbytes((tm, tn), BF16)]
    return pl.pallas_call(
        functools.partial(_outproj_kernel, final=final),
        grid=grid,
        in_specs=in_specs,
        out_specs=out_specs,
        out_shape=out_shape,
        scratch_shapes=scratch,
        compiler_params=pltpu.CompilerParams(
            dimension_semantics=("arbitrary", "arbitrary"),
            vmem_limit_bytes=_vmem_limit(*blocks, scratch=2 * _nbytes((tm, tn), F32)
                                         + sum(_nbytes(sc.shape, sc.dtype) for sc in scratch))),
        name="outproj_final" if final else "outproj",
    )(ya, yb, yc, w_bf, x2, g.reshape(1, D_MODEL))


def kernel(x, norm_g, w_in, conv_dw, conv_ln_g, conv_ln_b, conv_pw, attn_sinks, pool_w, pool_scale,
           w_out, final_norm_g):
    b, s, d = x.shape
    assert d == D_MODEL and w_in.shape[-1] == IN_WIDTH and s % 1024 == 0 and (b * s) % 2048 == 0
    depth = w_in.shape[0]
    t = b * s
    conv_pw_bf = conv_pw.astype(BF16)
    pool_w_bf = pool_w.astype(BF16)
    w_in_bf = [w_in[0:1].astype(BF16)]
    cast_jobs = [(w_in, l) for l in range(1, depth)] + [(w_out, l) for l in range(depth)]

    x2 = x.reshape(t, d)
    xg, r = _prenorm(x2, norm_g[0])
    for l in range(depth):
        proj, casted = _inproj(xg, w_in_bf[l], 0, r, cast_jobs if l == 0 else ())
        if l == 0:
            w_in_bf += list(casted[:depth - 1])
            w_out_bf = list(casted[depth - 1:])
        proj3 = proj.reshape(b, s, IN_WIDTH)
        ya, yc = _conv_pool_branch(proj3, conv_dw[l], conv_ln_g[l], conv_ln_b[l], conv_pw_bf[l],
                                   pool_w_bf[l], pool_scale[l])
        ya, yc = ya.reshape(t, CONV_WIDTH), yc.reshape(t, POOL_WIDTH)
        yb = _attn_branch(proj3, attn_sinks[l]).reshape(t, ATTN_WIDTH)
        if l + 1 < depth:
            x2, xg, r = _outproj(ya, yb, yc, w_out_bf[l], 0, x2, norm_g[l + 1], final=False, tm=1024, tn=512)
        else:
            x2 = _outproj(ya, yb, yc, w_out_bf[l], 0, x2, final_norm_g, final=True, tm=1024, tn=512)
    return x2.reshape(b, s, d)
```
